```python
import math
import jax, jax.numpy as jnp
from jax import lax
import numpy as np

D_MODEL = 1024
BATCH = 16
SEQ = 256
DEPTH = 4
DEC_BATCH = 8
DEC_SEQ = 4096
PAST_LEN = 512

GRID_W = 64
N_MIXERS = 3
N_MLA_LAYERS = (DEPTH + 2) // 3
N_SSD_LAYERS = (DEPTH + 1) // 3
N_MLSTM_LAYERS = DEPTH // 3
NORM_EPS = 1e-6
CONV_W = 4

MLA_HEADS = 16
MLA_NOPE = 64
MLA_ROPE = 32
MLA_V = 64
MLA_Q_RANK = 384
MLA_KV_RANK = 256
MLA_GATE = MLA_HEADS * MLA_V
MLA_IN = MLA_Q_RANK + MLA_KV_RANK + MLA_ROPE + MLA_GATE
MLA_SCALE = (MLA_NOPE + MLA_ROPE) ** -0.5
ROPE_BASE = 10000.0
Q_BLOCK = 128

SSD_INNER = 2 * D_MODEL
SSD_HEADDIM = 64
SSD_HEADS = SSD_INNER // SSD_HEADDIM
SSD_STATE = 128
SSD_GROUPS = 8
SSD_CONV_DIM = SSD_INNER + 2 * SSD_GROUPS * SSD_STATE
SSD_IN = SSD_INNER + SSD_CONV_DIM + 2 * SSD_HEADS
SSD_CHUNK = 128

ML_INNER = 2 * D_MODEL
ML_HEADS = 4
ML_DH = ML_INNER // ML_HEADS
ML_IN = 3 * ML_INNER + 4 * ML_HEADS
ML_CHUNK = 64

kernel_name = 'hybrid_mla_ssd_mlstm_flow_step'

F32 = jnp.float32


def rmsnorm(x, g):
    xf = x.astype(F32)
    y = xf * lax.rsqrt(jnp.mean(xf * xf, axis=-1, keepdims=True) + NORM_EPS)
    return (y * g.astype(F32)).astype(x.dtype)


def headwise_layernorm(h, g):
    hf = h.astype(F32)
    mu = jnp.mean(hf, axis=-1, keepdims=True)
    var = jnp.mean(jnp.square(hf - mu), axis=-1, keepdims=True)
    return ((hf - mu) * lax.rsqrt(var + NORM_EPS) * g.astype(F32)).astype(h.dtype)


def adaln(cond, w, b):
    m = (jax.nn.silu(cond) @ w + b)[:, None, :]
    return jnp.split(m, 3, axis=-1)


def modulate(x, g, shift, scale):
    return rmsnorm(x, g) * (1 + scale) + shift


def centred_dwconv(x, w, b):
    ch = x.shape[-1]
    y = lax.conv_general_dilated(x, w[:, None, :].astype(x.dtype), window_strides=(1,),
                                 padding=[((CONV_W - 1) // 2, CONV_W // 2)],
                                 dimension_numbers=('NWC', 'WIO', 'NWC'), feature_group_count=ch)
    return y + b


def to_chunks(t, size):
    return t.reshape(t.shape[0], t.shape[1] // size, size, *t.shape[2:]).swapaxes(0, 1)


def from_chunks(t):
    t = t.swapaxes(0, 1)
    return t.reshape(t.shape[0], t.shape[1] * t.shape[2], *t.shape[3:])


def flip(t):
    return jnp.flip(t, axis=1)


def axial_rope(n_tokens):
    rows = n_tokens // GRID_W
    row = jnp.repeat(jnp.arange(rows), GRID_W).astype(F32)
    col = jnp.tile(jnp.arange(GRID_W), rows).astype(F32)
    half = MLA_ROPE // 2
    inv = 1.0 / (ROPE_BASE ** (jnp.arange(0, half, 2, dtype=F32) / half))
    ang = jnp.concatenate([row[:, None] * inv, col[:, None] * inv], axis=-1)
    return jnp.cos(ang), jnp.sin(ang)


def apply_rope(x, cos, sin):
    cos = cos.astype(x.dtype)
    sin = sin.astype(x.dtype)
    x1, x2 = x[..., :MLA_ROPE // 2], x[..., MLA_ROPE // 2:]
    return jnp.concatenate([x1 * cos - x2 * sin, x2 * cos + x1 * sin], axis=-1)


def mla_expand(ckv, w_uk, w_uv):
    return (jnp.einsum('blc,chd->blhd', ckv, w_uk), jnp.einsum('blc,chd->blhd', ckv, w_uv))


def mla_attention(q_nope, q_pe, k_nope, k_pe, v):
    b, s, h, _ = q_nope.shape
    nb = s // Q_BLOCK

    def blocks(t):
        return t.reshape(b, nb, Q_BLOCK, *t.shape[2:]).swapaxes(0, 1)

    def one_block(qs):
        qn, qp = qs
        sc = jnp.einsum('bqhd,bkhd->bhqk', qn, k_nope) + jnp.einsum('bqhr,bkr->bhqk', qp, k_pe)
        p = jax.nn.softmax(sc.astype(F32) * MLA_SCALE, axis=-1).astype(v.dtype)
        return jnp.einsum('bhqk,bkhd->bqhd', p, v)

    o = lax.map(one_block, (blocks(q_nope), blocks(q_pe)))
    return o.swapaxes(0, 1).reshape(b, s, h * v.shape[-1])


def mla_mixer(h, w_in, q_norm_g, kv_norm_g, w_uq, w_uk, w_uv, w_out, rope, ctx_ckv, ctx_kpe):
    cq, ckv, kpe, z = jnp.split(h @ w_in, [MLA_Q_RANK, MLA_Q_RANK + MLA_KV_RANK,
                                           MLA_Q_RANK + MLA_KV_RANK + MLA_ROPE], axis=-1)
    q = jnp.einsum('blr,rhd->blhd', rmsnorm(cq, q_norm_g), w_uq)
    q_nope, q_pe = q[..., :MLA_NOPE], q[..., MLA_NOPE:]
    ckv = rmsnorm(ckv, kv_norm_g)
    if rope is not None:
        cos, sin = rope
        q_pe = apply_rope(q_pe, cos[:, None, :], sin[:, None, :])
        kpe_keys = apply_rope(kpe, cos, sin)
    else:
        kpe_keys = kpe
    k_nope, v = mla_expand(ckv, w_uk, w_uv)
    if ctx_ckv is not None:
        kc, vc = mla_expand(ctx_ckv, w_uk, w_uv)
        k_nope = jnp.concatenate([k_nope, kc], axis=1)
        v = jnp.concatenate([v, vc], axis=1)
        kpe_keys = jnp.concatenate([kpe_keys, ctx_kpe], axis=1)
    o = mla_attention(q_nope, q_pe, k_nope, kpe_keys, v)
    return (o * jax.nn.silu(z)) @ w_out, ckv, kpe


def ssd_scan(x, dt, a, bm, cm, h0):
    b, _, nh, p = x.shape
    g = bm.shape[2]
    r = nh // g
    q = SSD_CHUNK
    tri = jnp.tril(jnp.ones((q, q), bool))[None, :, :, None]

    def step(hs, inp):
        xc, dtc, bc, cc = inp
        acs = jnp.cumsum(dtc * a, axis=1)
        seg = acs[:, :, None, :] - acs[:, None, :, :]
        lmat = jnp.exp(jnp.where(tri, seg, -jnp.inf))
        cb = jnp.einsum('bign,bjgn->bijg', cc, bc).astype(F32)
        mm = (cb[..., None] * lmat.reshape(b, q, q, g, r) * dtc.reshape(b, 1, q, g, r)).astype(xc.dtype)
        xg = xc.reshape(b, q, g, r, p)
        y = jnp.einsum('bijgr,bjgrp->bigrp', mm, xg)
        hg = hs.reshape(b, g, r, p, -1)
        y = y + jnp.einsum('bign,bgrpn->bigrp', cc, hg) * jnp.exp(acs).reshape(b, q, g, r, 1).astype(xc.dtype)
        w_end = (jnp.exp(acs[:, -1:, :] - acs) * dtc).reshape(b, q, g, r).astype(xc.dtype)
        h_new = (hs * jnp.exp(acs[:, -1])[:, :, None, None].astype(hs.dtype)
                 + jnp.einsum('bjgn,bjgr,bjgrp->bgrpn', bc, w_end, xg).reshape(hs.shape))
        return h_new.astype(hs.dtype), y.reshape(b, q, nh, p)

    h_t, ys = lax.scan(step, h0, (to_chunks(x, q), to_chunks(dt, q), to_chunks(bm, q), to_chunks(cm, q)))
    return from_chunks(ys), h_t


def ssd_mixer(h, w_in, conv_w, conv_b, dt_bias, a_log, d_skip, norm_g, w_out, h0):
    b, n, _ = h.shape
    z, xbc, dt = jnp.split(h @ w_in, [SSD_INNER, SSD_INNER + SSD_CONV_DIM], axis=-1)
    xbc = jax.nn.silu(centred_dwconv(xbc, conv_w, conv_b))
    xs, bm, cm = jnp.split(xbc, [SSD_INNER, SSD_INNER + SSD_GROUPS * SSD_STATE], axis=-1)
    xs = xs.reshape(b, n, SSD_HEADS, SSD_HEADDIM)
    bm = bm.reshape(b, n, SSD_GROUPS, SSD_STATE)
    cm = cm.reshape(b, n, SSD_GROUPS, SSD_STATE)
    dt = jax.nn.softplus(dt.reshape(b, n, 2, SSD_HEADS).astype(F32) + dt_bias.astype(F32))
    a = -jnp.exp(a_log.astype(F32))
    y_f, h_f = ssd_scan(xs, dt[:, :, 0], a[0], bm, cm, h0[:, 0])
    y_b, h_b = ssd_scan(flip(xs), flip(dt[:, :, 1]), a[1], flip(bm), flip(cm), h0[:, 1])
    y = y_f + flip(y_b) + d_skip[:, None] * xs
    y = rmsnorm(y.reshape(b, n, SSD_INNER) * jax.nn.silu(z), norm_g)
    return y @ w_out, jnp.stack([h_f, h_b], axis=1)


def mlstm_scan(q, k, v, ig, lf, c0, n0, m0):
    ln = ML_CHUNK
    tri = jnp.tril(jnp.ones((ln, ln), bool))

    def step(carry, inp):
        cs, ns, ms = carry
        qc, kc, vc, ic, fc = inp
        ic = ic.swapaxes(1, 2)
        bcum = jnp.cumsum(fc.swapaxes(1, 2), axis=-1)
        mp = ms.astype(F32)
        dmat = jnp.where(tri, bcum[..., :, None] - bcum[..., None, :] + ic[..., None, :], -jnp.inf)
        m_inter = bcum + mp[..., None]
        m_i = jnp.maximum(jnp.max(dmat, axis=-1), m_inter)
        wmat = jnp.exp(dmat - m_i[..., None]).astype(qc.dtype)
        g_inter = jnp.exp(m_inter - m_i)
        s = jnp.einsum('bihk,bjhk->bhij', qc, kc) * wmat
        num = (jnp.einsum('bhij,bjhv->bihv', s, vc).astype(F32)
               + g_inter.swapaxes(1, 2)[..., None] * jnp.einsum('bihk,bhkv->bihv', qc, cs).astype(F32))
        den = jnp.sum(s, axis=-1).astype(F32) + g_inter * jnp.einsum('bihk,bhk->bhi', qc, ns).astype(F32)
        denom = jnp.maximum(jnp.abs(den), jnp.exp(-m_i))
        hc = (num / denom.swapaxes(1, 2)[..., None]).astype(qc.dtype)
        m_new = m_i[..., -1]
        w_end = jnp.exp(bcum[..., -1:] - bcum + ic - m_new[..., None]).astype(kc.dtype)
        decay = jnp.exp(bcum[..., -1] + mp - m_new)
        c_new = decay[..., None, None] * cs.astype(F32) + jnp.einsum('bhj,bjhk,bjhv->bhkv', w_end, kc, vc).astype(F32)
        n_new = decay[..., None] * ns.astype(F32) + jnp.einsum('bhj,bjhk->bhk', w_end, kc).astype(F32)
        return (c_new.astype(cs.dtype), n_new.astype(ns.dtype), m_new.astype(ms.dtype)), hc

    (c_t, n_t, m_t), hs = lax.scan(step, (c0, n0, m0),
                                   (to_chunks(q, ln), to_chunks(k, ln), to_chunks(v, ln),
                                    to_chunks(ig, ln), to_chunks(lf, ln)))
    return from_chunks(hs), c_t, n_t, m_t


def mlstm_mixer(h, w_in, gate_b, conv_w, conv_b, w_q, w_k, w_v, norm_g, skip, w_out, c0, n0, m0):
    b, n, _ = h.shape
    xm, o_pre, z, gates = jnp.split(h @ w_in, [ML_INNER, 2 * ML_INNER, 3 * ML_INNER], axis=-1)
    xc = jax.nn.silu(centred_dwconv(xm, conv_w, conv_b))
    q = jnp.einsum('blhd,hde->blhe', xc.reshape(b, n, ML_HEADS, ML_DH), w_q)
    k = jnp.einsum('blhd,hde->blhe', xc.reshape(b, n, ML_HEADS, ML_DH), w_k) * (ML_DH ** -0.5)
    v = jnp.einsum('blhd,hde->blhe', xm.reshape(b, n, ML_HEADS, ML_DH), w_v)
    gates = gates.reshape(b, n, 2, 2, ML_HEADS).astype(F32) + gate_b.astype(F32)
    ig = gates[:, :, :, 0]
    lf = jax.nn.log_sigmoid(gates[:, :, :, 1])
    h_f, c_f, n_f, m_f = mlstm_scan(q, k, v, ig[:, :, 0], lf[:, :, 0], c0[:, 0], n0[:, 0], m0[:, 0])
    h_b, c_b, n_b, m_b = mlstm_scan(flip(q), flip(k), flip(v), flip(ig[:, :, 1]), flip(lf[:, :, 1]),
                                    c0[:, 1], n0[:, 1], m0[:, 1])
    hh = (h_f + flip(h_b)) * jax.nn.sigmoid(o_pre).reshape(b, n, ML_HEADS, ML_DH)
    hh = headwise_layernorm(hh, norm_g).reshape(b, n, ML_INNER)
    out = ((hh + skip * xc) * jax.nn.silu(z)) @ w_out
    return (out, jnp.stack([c_f, c_b], axis=1), jnp.stack([n_f, n_b], axis=1), jnp.stack([m_f, m_b], axis=1))


def setup_inputs(seed: int = 0) -> dict:
    key = jax.random.key(seed)
    ks = iter(jax.random.split(key, 64))

    def nrm(shape, scale=1.0):
        return jax.random.normal(next(ks), shape, F32) * scale

    def gain(shape):
        return 1.0 + nrm(shape, 0.02)

    D = D_MODEL
    inp = {}
    inp['x_prompt'] = nrm((BATCH, SEQ, D))
    inp['x_sample'] = nrm((DEC_BATCH, DEC_SEQ, D))
    inp['cache_mla_ckv'] = nrm((DEC_BATCH, N_MLA_LAYERS, PAST_LEN, MLA_KV_RANK))
    inp['cache_mla_kpe'] = nrm((DEC_BATCH, N_MLA_LAYERS, PAST_LEN, MLA_ROPE))
    inp['state_ssd'] = nrm((DEC_BATCH, N_SSD_LAYERS, 2, SSD_HEADS, SSD_HEADDIM, SSD_STATE), 0.1)
    inp['state_mlstm_c'] = nrm((DEC_BATCH, N_MLSTM_LAYERS, 2, ML_HEADS, ML_DH, ML_DH), 0.05)
    inp['state_mlstm_n'] = nrm((DEC_BATCH, N_MLSTM_LAYERS, 2, ML_HEADS, ML_DH), 0.05)
    inp['state_mlstm_m'] = jax.random.uniform(next(ks), (DEC_BATCH, N_MLSTM_LAYERS, 2, ML_HEADS), F32, 0.0, 4.0)
    inp['c'] = nrm((DEC_BATCH, D))
    inp['c_ctx'] = nrm((D,))
    inp['ada_w'] = nrm((DEPTH, D, 3 * D), 0.5 * D ** -0.5)
    inp['ada_b'] = nrm((DEPTH, 3 * D), 0.02)
    inp['norm_g'] = gain((DEPTH, D))
    inp['final_norm_g'] = gain((D,))
    nm = N_MLA_LAYERS
    inp['mla_w_in'] = nrm((nm, D, MLA_IN), D ** -0.5)
    inp['mla_q_norm_g'] = gain((nm, MLA_Q_RANK))
    inp['mla_kv_norm_g'] = gain((nm, MLA_KV_RANK))
    inp['mla_w_uq'] = nrm((nm, MLA_Q_RANK, MLA_HEADS, MLA_NOPE + MLA_ROPE), MLA_Q_RANK ** -0.5)
    inp['mla_w_uk'] = nrm((nm, MLA_KV_RANK, MLA_HEADS, MLA_NOPE), MLA_KV_RANK ** -0.5)
    inp['mla_w_uv'] = nrm((nm, MLA_KV_RANK, MLA_HEADS, MLA_V), MLA_KV_RANK ** -0.5)
    inp['mla_w_out'] = nrm((nm, MLA_GATE, D), MLA_GATE ** -0.5)
    ns_ = N_SSD_LAYERS
    inp['ssd_w_in'] = nrm((ns_, D, SSD_IN), D ** -0.5)
    inp['ssd_conv_w'] = nrm((ns_, CONV_W, SSD_CONV_DIM), CONV_W ** -0.5)
    inp['ssd_conv_b'] = nrm((ns_, SSD_CONV_DIM), 0.02)
    u = jax.random.uniform(next(ks), (ns_, 2, SSD_HEADS), F32)
    dt0 = jnp.exp(u * (math.log(0.1) - math.log(0.001)) + math.log(0.001))
    inp['ssd_dt_bias'] = dt0 + jnp.log(-jnp.expm1(-dt0))
    inp['ssd_a_log'] = jnp.log(jax.random.uniform(next(ks), (ns_, 2, SSD_HEADS), F32, 1.0, 16.0))
    inp['ssd_d'] = 1.0 + nrm((ns_, SSD_HEADS), 0.1)
    inp['ssd_norm_g'] = gain((ns_, SSD_INNER))
    inp['ssd_w_out'] = nrm((ns_, SSD_INNER, D), SSD_INNER ** -0.5)
    nl = N_MLSTM_LAYERS
    inp['ml_w_in'] = nrm((nl, D, ML_IN), D ** -0.5)
    ib = nrm((nl, 2, 1, ML_HEADS), 0.1)
    fb = jnp.linspace(3.0, 6.0, ML_HEADS, dtype=F32) + nrm((nl, 2, 1, ML_HEADS), 0.1)
    inp['ml_gate_b'] = jnp.concatenate([ib, fb], axis=2)
    inp['ml_conv_w'] = nrm((nl, CONV_W, ML_INNER), CONV_W ** -0.5)
    inp['ml_conv_b'] = nrm((nl, ML_INNER), 0.02)
    inp['ml_w_q'] = nrm((nl, ML_HEADS, ML_DH, ML_DH), ML_DH ** -0.5)
    inp['ml_w_k'] = nrm((nl, ML_HEADS, ML_DH, ML_DH), ML_DH ** -0.5)
    inp['ml_w_v'] = nrm((nl, ML_HEADS, ML_DH, ML_DH), ML_DH ** -0.5)
    inp['ml_norm_g'] = gain((nl, ML_HEADS, ML_DH))
    inp['ml_skip'] = 1.0 + nrm((nl, ML_INNER), 0.1)
    inp['ml_w_out'] = nrm((nl, ML_INNER, D), ML_INNER ** -0.5)
    return inp


def reference(x_prompt, x_sample, cache_mla_ckv, cache_mla_kpe, state_ssd, state_mlstm_c, state_mlstm_n,
              state_mlstm_m, c, c_ctx, ada_w, ada_b, norm_g, final_norm_g,
              mla_w_in, mla_q_norm_g, mla_kv_norm_g, mla_w_uq, mla_w_uk, mla_w_uv, mla_w_out,
              ssd_w_in, ssd_conv_w, ssd_conv_b, ssd_dt_bias, ssd_a_log, ssd_d, ssd_norm_g, ssd_w_out,
              ml_w_in, ml_gate_b, ml_conv_w, ml_conv_b, ml_w_q, ml_w_k, ml_w_v, ml_norm_g, ml_skip, ml_w_out):
    bp = x_prompt.shape[0]
    rope = axial_rope(x_sample.shape[1])
    xp, xs = x_prompt, x_sample
    new_ckv, new_kpe, new_ssd, new_c, new_n, new_m = [], [], [], [], [], []
    for i in range(DEPTH):
        kind, j = i % N_MIXERS, i // N_MIXERS
        sh_p, sc_p, g_p = adaln(c_ctx[None, :], ada_w[i], ada_b[i])
        sh_s, sc_s, g_s = adaln(c, ada_w[i], ada_b[i])
        hp = modulate(xp, norm_g[i], sh_p, sc_p)
        hs = modulate(xs, norm_g[i], sh_s, sc_s)
        if kind == 0:
            wts = (mla_w_in[j], mla_q_norm_g[j], mla_kv_norm_g[j], mla_w_uq[j], mla_w_uk[j], mla_w_uv[j], mla_w_out[j])
            out_p, ckv, kpe = mla_mixer(hp, *wts, None, None, None)
            out_s, _, _ = mla_mixer(hs, *wts, rope, cache_mla_ckv[:, j], cache_mla_kpe[:, j])
            new_ckv.append(ckv)
            new_kpe.append(kpe)
        elif kind == 1:
            wts = (ssd_w_in[j], ssd_conv_w[j], ssd_conv_b[j], ssd_dt_bias[j], ssd_a_log[j], ssd_d[j],
                   ssd_norm_g[j], ssd_w_out[j])
            h0 = jnp.zeros((bp, 2, SSD_HEADS, SSD_HEADDIM, SSD_STATE), xp.dtype)
            out_p, st = ssd_mixer(hp, *wts, h0)
            out_s, _ = ssd_mixer(hs, *wts, state_ssd[:, j])
            new_ssd.append(st)
        else:
            wts = (ml_w_in[j], ml_gate_b[j], ml_conv_w[j], ml_conv_b[j], ml_w_q[j], ml_w_k[j], ml_w_v[j],
                   ml_norm_g[j], ml_skip[j], ml_w_out[j])
            c0 = jnp.zeros((bp, 2, ML_HEADS, ML_DH, ML_DH), xp.dtype)
            n0 = jnp.zeros((bp, 2, ML_HEADS, ML_DH), xp.dtype)
            m0 = jnp.zeros((bp, 2, ML_HEADS), xp.dtype)
            out_p, cc, nn_, mm = mlstm_mixer(hp, *wts, c0, n0, m0)
            out_s, _, _, _ = mlstm_mixer(hs, *wts, state_mlstm_c[:, j], state_mlstm_n[:, j], state_mlstm_m[:, j])
            new_c.append(cc)
            new_n.append(nn_)
            new_m.append(mm)
        xp = xp + g_p * out_p
        xs = xs + g_s * out_s
    y_prompt = rmsnorm(xp, final_norm_g)
    y_sample = rmsnorm(xs, final_norm_g)
    return (y_prompt, y_sample, jnp.stack(new_ckv, axis=1), jnp.stack(new_kpe, axis=1), jnp.stack(new_ssd, axis=1),
            jnp.stack(new_c, axis=1), jnp.stack(new_n, axis=1), jnp.stack(new_m, axis=1))
```

```python
import functools

import jax
import jax.numpy as jnp
from jax import lax
from jax.experimental import pallas as pl
from jax.experimental.pallas import tpu as pltpu

F32 = jnp.float32
BF16 = jnp.bfloat16
HIGHEST = lax.Precision.HIGHEST

D = 1024
DEPTH = 4
GRID_W = 64
N_MIXERS = 3
NORM_EPS = 1e-6
CONV_W = 4

MLA_HEADS = 16
MLA_NOPE = 64
MLA_ROPE = 32
MLA_V = 64
MLA_Q_RANK = 384
MLA_KV_RANK = 256
MLA_SCALE = (MLA_NOPE + MLA_ROPE) ** -0.5
ROPE_BASE = 10000.0

SSD_INNER = 2 * D
SSD_HEADDIM = 64
SSD_HEADS = SSD_INNER // SSD_HEADDIM
SSD_STATE = 128
SSD_GROUPS = 8
SSD_BC = SSD_GROUPS * SSD_STATE
SSD_CONV_DIM = SSD_INNER + 2 * SSD_BC
SSD_CHUNK = 128

ML_INNER = 2 * D
ML_HEADS = 4
ML_DH = ML_INNER // ML_HEADS

LANES = 128
SUBLANES = 8
VMEM_LIMIT = 56 * 1024 * 1024

TOKEN_TILE = 256
HALO = SUBLANES
ML_CHUNK = 256
ATTN_TQ = 512
ATTN_TK = 512


def _cparams(*sem):
    return pltpu.CompilerParams(dimension_semantics=sem, vmem_limit_bytes=VMEM_LIMIT)


def _silu(v):
    return v * jax.nn.sigmoid(v)


def _softplus(v):
    return jnp.maximum(v, 0.0) + jnp.log(1.0 + jnp.exp(-jnp.abs(v)))


def _log_sigmoid(v):
    return jnp.minimum(v, 0.0) - jnp.log(1.0 + jnp.exp(-jnp.abs(v)))


def _modulate(x, g, mod):
    ms = jnp.mean(x * x, axis=-1, keepdims=True)
    y = x * lax.rsqrt(ms + NORM_EPS) * g
    return y * (1.0 + mod[:, D:2 * D]) + mod[:, 0:D]


def _dot(a, b):
    return jnp.dot(a, b, preferred_element_type=F32)


def _dot_nt(a, b):
    return lax.dot_general(a, b, (((1,), (1,)), ((), ())), preferred_element_type=F32)


def _const_spec(shape):
    nd = len(shape)
    return pl.BlockSpec(shape, lambda *_: (0,) * nd)


def _mod_spec(per_batch):
    if per_batch:
        return pl.BlockSpec((1, 1, 3 * D), lambda b, i: (b, 0, 0))
    return pl.BlockSpec((1, 1, 3 * D), lambda b, i: (0, 0, 0))


def _adaln_body(cond_ref, w_ref, b_ref, o_ref):
    a = _silu(cond_ref[...]).astype(BF16)
    o_ref[0] = _dot(a, w_ref[0].astype(BF16)) + b_ref[0]


def adaln_all(cond, ada_w, ada_b):
    r = cond.shape[0]
    return pl.pallas_call(
        _adaln_body,
        grid=(DEPTH, 3),
        in_specs=[pl.BlockSpec((r, D), lambda i, j: (0, 0)),
                  pl.BlockSpec((1, D, D), lambda i, j: (i, 0, j)),
                  pl.BlockSpec((1, 1, D), lambda i, j: (i, 0, j))],
        out_specs=pl.BlockSpec((1, r, D), lambda i, j: (i, 0, j)),
        out_shape=jax.ShapeDtypeStruct((DEPTH, r, 3 * D), F32),
        compiler_params=_cparams("arbitrary", "arbitrary"),
        name="adaln",
    )(cond, ada_w, ada_b.reshape(DEPTH, 1, 3 * D))


class _Piece:
    def __init__(self, width, transposed=False, bias=False, act=None):
        self.width, self.transposed, self.bias, self.act = width, transposed, bias, act


def _inproj_body(x_ref, g_ref, mod_ref, *rest, pieces):
    n_in = sum(2 if p.bias else 1 for p in pieces)
    ins, outs = rest[:n_in], rest[n_in:]
    h = _modulate(x_ref[0], g_ref[...], mod_ref[0]).astype(BF16)
    k = 0
    for p, o_ref in zip(pieces, outs):
        w_ref = ins[k]
        k += 1
        b_ref = None
        if p.bias:
            b_ref = ins[k]
            k += 1
        if p.transposed:
            r = _dot_nt(w_ref[...], h)
            if b_ref is not None:
                r = r + b_ref[...]
            if p.act is not None:
                r = p.act(r, True)
            o_ref[0] = r.astype(o_ref.dtype)
        else:
            for c0 in range(0, p.width, 1024):
                c1 = min(p.width, c0 + 1024)
                r = _dot(h, w_ref[:, c0:c1])
                if b_ref is not None:
                    r = r + b_ref[:, c0:c1]
                if p.act is not None:
                    r = p.act(r, False)
                o_ref[0, :, c0:c1] = r.astype(o_ref.dtype)


def inproj(x, g, mod, pieces, operands, per_batch):
    bsz, seq, _ = x.shape
    tm = TOKEN_TILE
    in_specs = [pl.BlockSpec((1, tm, D), lambda b, i: (b, i, 0)), _const_spec((1, D)), _mod_spec(per_batch)]
    in_specs += [_const_spec(op.shape) for op in operands]
    out_specs, out_shapes = [], []
    for p in pieces:
        if p.transposed:
            out_specs.append(pl.BlockSpec((1, p.width, tm), lambda b, i: (b, 0, i)))
            out_shapes.append(jax.ShapeDtypeStruct((bsz, p.width, seq), F32))
        else:
            out_specs.append(pl.BlockSpec((1, tm, p.width), lambda b, i: (b, i, 0)))
            out_shapes.append(jax.ShapeDtypeStruct((bsz, seq, p.width), F32))
    return pl.pallas_call(
        functools.partial(_inproj_body, pieces=pieces),
        grid=(bsz, seq // tm),
        in_specs=in_specs, out_specs=out_specs, out_shape=out_shapes,
        compiler_params=_cparams("parallel", "parallel"),
        name="inproj",
    )(x, g.reshape(1, D), mod, *operands)


def _halo_specs(tm, width, seq):
    r = tm // HALO
    last = seq // HALO - 1
    return [pl.BlockSpec((1, tm, width), lambda b, i: (b, i, 0)),
            pl.BlockSpec((1, HALO, width), lambda b, i: (b, jnp.maximum(i * r - 1, 0), 0)),
            pl.BlockSpec((1, HALO, width), lambda b, i: (b, jnp.minimum((i + 1) * r, last), 0))]


def _conv_silu(main, prev8, next8, w, b, first, last):
    tm = main.shape[0]
    row = lax.broadcasted_iota(jnp.int32, main.shape, 0)
    before = jnp.where(first, 0.0, prev8[HALO - 1:HALO, :])
    after0 = jnp.where(last, 0.0, next8[0:1, :])
    after1 = jnp.where(last, 0.0, next8[1:2, :])
    xm1 = jnp.where(row == 0, before, pltpu.roll(main, 1, 0))
    xp1 = jnp.where(row == tm - 1, after0, pltpu.roll(main, tm - 1, 0))
    xp2 = jnp.where(row == tm - 2, after0, jnp.where(row == tm - 1, after1, pltpu.roll(main, tm - 2, 0)))
    y = w[0:1, :] * xm1 + w[1:2, :] * main + w[2:3, :] * xp1 + w[3:4, :] * xp2 + b
    return _silu(y)


def _rms(v, g):
    return v * lax.rsqrt(jnp.mean(v * v, axis=-1, keepdims=True) + NORM_EPS) * g


def _outproj_body(*refs, kind, final_norm):
    if kind == "mla":
        o_ref, z_ref, x_ref, mod_ref, w_ref = refs[:5]
        rest = refs[5:]
        pre = o_ref[0] * _silu(z_ref[0])
    elif kind == "ssd":
        yf_ref, yb_ref, xs_ref, z_ref, x_ref, mod_ref, w_ref, dsk_ref, ng_ref = refs[:9]
        rest = refs[9:]
        y = yf_ref[0, 0] + yb_ref[0, 0] + dsk_ref[...] * xs_ref[0]
        pre = _rms(y * _silu(z_ref[0]), ng_ref[...])
    else:
        hf_ref, hb_ref, op_ref, xc_ref, z_ref, x_ref, mod_ref, w_ref, ng_ref, sk_ref = refs[:10]
        rest = refs[10:]
        hh = (hf_ref[0, 0] + hb_ref[0, 0]) * jax.nn.sigmoid(op_ref[0])
        parts = []
        for h in range(ML_HEADS):
            v = hh[:, h * ML_DH:(h + 1) * ML_DH]
            mu = jnp.mean(v, axis=-1, keepdims=True)
            vc = v - mu
            var = jnp.mean(vc * vc, axis=-1, keepdims=True)
            parts.append(vc * lax.rsqrt(var + NORM_EPS))
        hn = jnp.concatenate(parts, axis=-1) * ng_ref[...]
        pre = (hn + sk_ref[...] * xc_ref[0]) * _silu(z_ref[0])
    out = _dot(pre.astype(BF16), w_ref[...])
    xn = x_ref[0] + mod_ref[0][:, 2 * D:3 * D] * out
    if final_norm:
        fg_ref, xo_ref, yo_ref = rest
        xo_ref[0] = xn
        yo_ref[0] = _rms(xn, fg_ref[...])
    else:
        (xo_ref,) = rest
        xo_ref[0] = xn


def outproj(kind, x, mod, per_batch, w_out, tok_inputs, dir_inputs=(), consts=(), final_g=None):
    bsz, seq, _ = x.shape
    tm = TOKEN_TILE
    args, in_specs = [], []
    for a in dir_inputs:
        wd = a.shape[-1]
        for d in range(2):
            args.append(a)
            in_specs.append(pl.BlockSpec((1, 1, tm, wd), functools.partial(lambda b, i, d: (d, b, i, 0), d=d)))
    for a in tok_inputs:
        args.append(a)
        in_specs.append(pl.BlockSpec((1, tm, a.shape[-1]), lambda b, i: (b, i, 0)))
    args += [x, mod, w_out]
    in_specs += [pl.BlockSpec((1, tm, D), lambda b, i: (b, i, 0)), _mod_spec(per_batch), _const_spec(w_out.shape)]
    for a in consts:
        args.append(a)
        in_specs.append(_const_spec(a.shape))
    tok_spec = pl.BlockSpec((1, tm, D), lambda b, i: (b, i, 0))
    shp = jax.ShapeDtypeStruct((bsz, seq, D), F32)
    if final_g is not None:
        args.append(final_g.reshape(1, D))
        in_specs.append(_const_spec((1, D)))
        out_specs, out_shape = [tok_spec, tok_spec], [shp, shp]
    else:
        out_specs, out_shape = [tok_spec], [shp]
    res = pl.pallas_call(
        functools.partial(_outproj_body, kind=kind, final_norm=final_g is not None),
        grid=(bsz, seq // tm),
        in_specs=in_specs, out_specs=out_specs, out_shape=out_shape,
        compiler_params=_cparams("parallel", "parallel"),
        name="outproj_" + kind,
    )(*args)
    return res if final_g is not None else res[0]


def _mla_pre_body(x_ref, g_ref, mod_ref, wcq_ref, wckv_ref, wkpe_ref, wkpes_ref, wz_ref, qg_ref, kvg_ref,
                  wuq1_ref, wuq2_ref, cq_ref, sq_ref, ck_ref, sk_ref,
                  q_ref, ckv_ref, kraw_ref, krope_ref, z_ref):
    h = _modulate(x_ref[0], g_ref[...], mod_ref[0]).astype(BF16)
    z_ref[0] = _dot(h, wz_ref[...])
    ckv_ref[0] = _rms(_dot(h, wckv_ref[...]), kvg_ref[...])
    kraw = _dot(h, wkpe_ref[...])
    kraw_ref[0] = kraw
    krope_ref[0] = kraw * ck_ref[...] + _dot(h, wkpes_ref[...]) * sk_ref[...]
    cq = _rms(_dot(h, wcq_ref[...]), qg_ref[...]).astype(BF16)
    cosq = cq_ref[...]
    sinq = sq_ref[...]
    for c0 in range(0, MLA_HEADS * LANES, 512):
        a = _dot(cq, wuq1_ref[:, c0:c0 + 512])
        b = _dot(cq, wuq2_ref[:, c0:c0 + 512])
        for s in range(0, 512, LANES):
            q_ref[0, :, c0 + s:c0 + s + LANES] = (
                a[:, s:s + LANES] * cosq + b[:, s:s + LANES] * sinq).astype(BF16)


def _head_slabs(w, off=0):
    k, nh, dh = w.shape
    return jnp.pad(w, ((0, 0), (0, 0), (off, LANES - off - dh))).reshape(k, nh * LANES)


def _swap_halves(w):
    half = w.shape[-1] // 2
    return jnp.concatenate([w[..., half:], w[..., :half]], axis=-1)


def _rope_tables(seq, use_rope):
    zeros32 = jnp.zeros((seq, LANES - MLA_NOPE - MLA_ROPE), F32)
    if use_rope:
        rows = seq // GRID_W
        row = jnp.repeat(jnp.arange(rows), GRID_W).astype(F32)
        col = jnp.tile(jnp.arange(GRID_W), rows).astype(F32)
        half = MLA_ROPE // 2
        inv = 1.0 / (ROPE_BASE ** (jnp.arange(0, half, 2, dtype=F32) / half))
        ang = jnp.concatenate([row[:, None] * inv, col[:, None] * inv], axis=-1)
        cos, sin = jnp.cos(ang), jnp.sin(ang)
    else:
        cos = jnp.ones((seq, MLA_ROPE // 2), F32)
        sin = jnp.zeros((seq, MLA_ROPE // 2), F32)
    ones64 = jnp.ones((seq, MLA_NOPE), F32)
    zeros64 = jnp.zeros((seq, MLA_NOPE), F32)
    cos_k = jnp.concatenate([zeros64, cos, cos, zeros32], axis=-1)
    sin_k = jnp.concatenate([zeros64, -sin, sin, zeros32], axis=-1)
    cos_q = jnp.concatenate([ones64, cos, cos, zeros32], axis=-1) * MLA_SCALE
    sin_q = sin_k * MLA_SCALE
    return cos_q, sin_q, cos_k, sin_k


def mla_pre(x, g, mod, per_batch, wts, tables):
    bsz, seq, _ = x.shape
    tm = TOKEN_TILE
    tok = lambda w, dt: (pl.BlockSpec((1, tm, w), lambda b, i: (b, i, 0)), jax.ShapeDtypeStruct((bsz, seq, w), dt))
    outs = [tok(MLA_HEADS * LANES, BF16), tok(MLA_KV_RANK, F32), tok(LANES, F32), tok(LANES, F32), tok(D, F32)]
    tab_spec = pl.BlockSpec((tm, LANES), lambda b, i: (i, 0))
    in_specs = ([pl.BlockSpec((1, tm, D), lambda b, i: (b, i, 0)), _const_spec((1, D)), _mod_spec(per_batch)]
                + [_const_spec(w.shape) for w in wts] + [tab_spec] * 4)
    return pl.pallas_call(
        _mla_pre_body,
        grid=(bsz, seq // tm),
        in_specs=in_specs,
        out_specs=[o[0] for o in outs], out_shape=[o[1] for o in outs],
        compiler_params=_cparams("parallel", "parallel"),
        name="mla_pre",
    )(x, g.reshape(1, D), mod, *wts, *tables)


def _kv_expand_body(ckv_ref, kpe_ref, wk_ref, wv_ref, k_ref, v_ref):
    c = ckv_ref[0].astype(BF16)
    v_ref[0] = _dot(c, wv_ref[...]).astype(BF16)
    kpe = kpe_ref[0]
    for c0 in range(0, MLA_HEADS * LANES, 512):
        a = _dot(c, wk_ref[:, c0:c0 + 512])
        for s in range(0, 512, LANES):
            k_ref[0, :, c0 + s:c0 + s + LANES] = (a[:, s:s + LANES] + kpe).astype(BF16)


def kv_expand(ckv, kpe_slab, wk_aug, wv):
    bsz, seq, _ = ckv.shape
    tm = TOKEN_TILE
    return pl.pallas_call(
        _kv_expand_body,
        grid=(bsz, seq // tm),
        in_specs=[pl.BlockSpec((1, tm, MLA_KV_RANK), lambda b, i: (b, i, 0)),
                  pl.BlockSpec((1, tm, LANES), lambda b, i: (b, i, 0)),
                  _const_spec(wk_aug.shape), _const_spec(wv.shape)],
        out_specs=[pl.BlockSpec((1, tm, MLA_HEADS * LANES), lambda b, i: (b, i, 0)),
                   pl.BlockSpec((1, tm, MLA_HEADS * MLA_V), lambda b, i: (b, i, 0))],
        out_shape=[jax.ShapeDtypeStruct((bsz, seq, MLA_HEADS * LANES), BF16),
                   jax.ShapeDtypeStruct((bsz, seq, MLA_HEADS * MLA_V), BF16)],
        compiler_params=_cparams("parallel", "parallel"),
        name="mla_kv_expand",
    )(ckv, kpe_slab, wk_aug, wv)


def _attn_body(q_ref, k_ref, v_ref, o_ref, *, tk, nk):
    tq = q_ref.shape[1]
    left = lax.broadcasted_iota(jnp.int32, (tq, LANES), 1) < MLA_V

    def step(kc, carry):
        m0, l0, m1, l1, acc = carry
        ks = pl.multiple_of(kc * tk, tk)
        vv = v_ref[0, pl.ds(ks, tk), :]
        new, pv, alphas = [], [], []
        for hh, (m, l) in enumerate(((m0, l0), (m1, l1))):
            s = _dot_nt(q_ref[0, :, hh * LANES:(hh + 1) * LANES], k_ref[0, pl.ds(ks, tk), hh * LANES:(hh + 1) * LANES])
            mn = jnp.maximum(m, jnp.max(s, axis=-1, keepdims=True))
            p = jnp.exp(s - mn)
            alpha = jnp.exp(m - mn)
            new += [mn, alpha * l + jnp.sum(p, axis=-1, keepdims=True)]
            pv.append(_dot(p.astype(BF16), vv))
            alphas.append(alpha)
        acc = jnp.where(left, alphas[0] * acc + pv[0], alphas[1] * acc + pv[1])
        return new[0], new[1], new[2], new[3], acc

    neg = jnp.full((tq, 1), -jnp.inf, F32)
    zero = jnp.zeros((tq, 1), F32)
    m0, l0, m1, l1, acc = lax.fori_loop(0, nk, step, (neg, zero, neg, zero, jnp.zeros((tq, LANES), F32)))
    o_ref[0] = acc / jnp.where(left, l0, l1)


def mla_attention(q_aug, k_aug, v):
    bsz, seq, _ = q_aug.shape
    lk = k_aug.shape[1]
    tq = min(ATTN_TQ, seq)
    tk = min(ATTN_TK, lk)
    return pl.pallas_call(
        functools.partial(_attn_body, tk=tk, nk=lk // tk),
        grid=(bsz, MLA_HEADS // 2, seq // tq),
        in_specs=[pl.BlockSpec((1, tq, 2 * LANES), lambda b, h, i: (b, i, h)),
                  pl.BlockSpec((1, lk, 2 * LANES), lambda b, h, i: (b, 0, h)),
                  pl.BlockSpec((1, lk, LANES), lambda b, h, i: (b, 0, h))],
        out_specs=pl.BlockSpec((1, tq, LANES), lambda b, h, i: (b, i, h)),
        out_shape=jax.ShapeDtypeStruct((bsz, seq, MLA_HEADS * MLA_V), F32),
        compiler_params=_cparams("parallel", "parallel", "arbitrary"),
        name="mla_attention",
    )(q_aug, k_aug, v)


def _mla_weights(w_in, q_norm_g, kv_norm_g, w_uq, w_uk, w_uv):
    o1, o2, o3 = MLA_Q_RANK, MLA_Q_RANK + MLA_KV_RANK, MLA_Q_RANK + MLA_KV_RANK + MLA_ROPE
    w_kpe = w_in[:, o2:o3]
    pad_k = lambda w: jnp.pad(w, ((0, 0), (MLA_NOPE, LANES - MLA_NOPE - MLA_ROPE)))
    uq_pe = w_uq[..., MLA_NOPE:]
    wuq1 = _head_slabs(w_uq)
    wuq2 = _head_slabs(_swap_halves(uq_pe), off=MLA_NOPE)
    pre = (w_in[:, :o1].astype(BF16), w_in[:, o1:o2].astype(BF16), pad_k(w_kpe).astype(BF16),
           pad_k(_swap_halves(w_kpe)).astype(BF16), w_in[:, o3:].astype(BF16),
           q_norm_g.reshape(1, -1), kv_norm_g.reshape(1, -1), wuq1.astype(BF16), wuq2.astype(BF16))
    wk_aug = _head_slabs(w_uk).astype(BF16)
    wv = w_uv.reshape(MLA_KV_RANK, MLA_HEADS * MLA_V).astype(BF16)
    return pre, wk_aug, wv


def mla_layer(x, g, mod, per_batch, wts, w_out, use_rope, ctx=None, final_g=None):
    pre_w, wk_aug, wv = wts
    seq = x.shape[1]
    q_aug, ckv, kraw, krope, z = mla_pre(x, g, mod, per_batch, pre_w, _rope_tables(seq, use_rope))
    ckv_all, kpe_all = ckv, krope
    if ctx is not None:
        ctx_ckv, ctx_kpe = ctx
        ckv_all = jnp.concatenate([ckv, ctx_ckv], axis=1)
        kpe_all = jnp.concatenate(
            [krope, jnp.pad(ctx_kpe, ((0, 0), (0, 0), (MLA_NOPE, LANES - MLA_NOPE - MLA_ROPE)))], axis=1)
    k_aug, v = kv_expand(ckv_all, kpe_all, wk_aug, wv)
    o = mla_attention(q_aug, k_aug, v)
    res = outproj("mla", x, mod, per_batch, w_out, [o, z], final_g=final_g)
    return res, ckv, kraw[..., MLA_NOPE:MLA_NOPE + MLA_ROPE]


def _ssd_conv_body(x_ref, p_ref, n_ref, w_ref, b_ref, xs_ref, bt_ref, c_ref):
    i = pl.program_id(1)
    first = i == 0
    last = i == pl.num_programs(1) - 1
    cw = 512
    for c0 in range(0, SSD_CONV_DIM, cw):
        y = _conv_silu(x_ref[0, :, c0:c0 + cw], p_ref[0, :, c0:c0 + cw], n_ref[0, :, c0:c0 + cw],
                       w_ref[:, c0:c0 + cw], b_ref[:, c0:c0 + cw], first, last)
        if c0 < SSD_INNER:
            xs_ref[0, :, c0:c0 + cw] = y
        elif c0 < SSD_INNER + SSD_BC:
            o = c0 - SSD_INNER
            bt_ref[0, o:o + cw, :] = y.T.astype(BF16)
        else:
            o = c0 - SSD_INNER - SSD_BC
            c_ref[0, :, o:o + cw] = y.astype(BF16)


def ssd_conv(xbc, conv_w, conv_b):
    bsz, seq, _ = xbc.shape
    tm = TOKEN_TILE
    return pl.pallas_call(
        _ssd_conv_body,
        grid=(bsz, seq // tm),
        in_specs=_halo_specs(tm, SSD_CONV_DIM, seq) + [_const_spec(conv_w.shape), _const_spec((1, SSD_CONV_DIM))],
        out_specs=[pl.BlockSpec((1, tm, SSD_INNER), lambda b, i: (b, i, 0)),
                   pl.BlockSpec((1, SSD_BC, tm), lambda b, i: (b, 0, i)),
                   pl.BlockSpec((1, tm, SSD_BC), lambda b, i: (b, i, 0))],
        out_shape=[jax.ShapeDtypeStruct((bsz, seq, SSD_INNER), F32),
                   jax.ShapeDtypeStruct((bsz, SSD_BC, seq), BF16),
                   jax.ShapeDtypeStruct((bsz, seq, SSD_BC), BF16)],
        compiler_params=_cparams("parallel", "parallel"),
        name="ssd_conv",
    )(xbc, xbc, xbc, conv_w, conv_b.reshape(1, SSD_CONV_DIM))


def _scan_masks(d, n):
    ri = lax.broadcasted_iota(jnp.int32, (n, n), 0)
    ci = lax.broadcasted_iota(jnp.int32, (n, n), 1)
    diff = (ri - ci) * (1 - 2 * d)
    return diff >= 0, diff <= 0


def _ssd_scan_body(*refs, has_h0):
    if has_h0:
        xs_ref, bt_ref, c_ref, dt_ref, dtt_ref, a_ref, at_ref, h0_ref, y_ref, hout_ref, ht_s = refs
    else:
        xs_ref, bt_ref, c_ref, dt_ref, dtt_ref, a_ref, at_ref, y_ref, hout_ref, ht_s = refs
    d = pl.program_id(1)
    c = pl.program_id(2)
    q = SSD_CHUNK
    hp_n = SSD_HEADS * SSD_HEADDIM

    @pl.when(c == 0)
    def _():
        if has_h0:
            ht_s[...] = h0_ref[0, 0].reshape(hp_n, SSD_STATE).T
        else:
            ht_s[...] = jnp.zeros_like(ht_s)

    mask, mask_t = _scan_masks(d, q)
    dt = dt_ref[0, 0]
    dtt = dtt_ref[0]
    da = dt * a_ref[0]
    dat = dtt * at_ref[0]
    cum = jnp.dot(mask.astype(F32), da, precision=HIGHEST, preferred_element_type=F32)
    cum_t = jnp.dot(dat, mask_t.astype(F32), precision=HIGHEST, preferred_element_type=F32)
    tot = jnp.sum(da, axis=0, keepdims=True)
    exp_cum = jnp.exp(cum)
    w_end = jnp.exp(tot - cum) * dt
    decay = jnp.exp(tot)
    left = lax.broadcasted_iota(jnp.int32, (q, LANES), 1) < SSD_HEADDIM
    left_n = lax.broadcasted_iota(jnp.int32, (SSD_STATE, LANES), 1) < SSD_HEADDIM
    gw = (SSD_HEADS // SSD_GROUPS) * SSD_HEADDIM

    def pick(arr, h, cond):
        return jnp.where(cond, arr[:, h:h + 1], arr[:, h + 1:h + 2])

    for g in range(SSD_GROUPS):
        cg = c_ref[0, :, g * SSD_STATE:(g + 1) * SSD_STATE]
        btg = bt_ref[0, g * SSD_STATE:(g + 1) * SSD_STATE, :]
        cb = _dot(cg, btg)
        y_state = _dot(cg, ht_s[:, g * gw:(g + 1) * gw].astype(BF16))
        for pr in range(gw // LANES):
            lo = g * gw + pr * LANES
            h0 = lo // SSD_HEADDIM
            xs = xs_ref[0, :, lo:lo + LANES]
            xs_b = xs.astype(BF16)
            ys = []
            for h in (h0, h0 + 1):
                seg = cum[:, h:h + 1] - cum_t[h:h + 1, :]
                lm = jnp.exp(jnp.where(mask, seg, -jnp.inf))
                mm = (cb * lm * dtt[h:h + 1, :]).astype(BF16)
                ys.append(_dot(mm, xs_b))
            y_ref[0, 0, :, lo:lo + LANES] = (jnp.where(left, ys[0], ys[1])
                                             + y_state[:, pr * LANES:(pr + 1) * LANES] * pick(exp_cum, h0, left))
            xw = (xs * pick(w_end, h0, left)).astype(BF16)
            ht_s[:, lo:lo + LANES] = ht_s[:, lo:lo + LANES] * pick(decay, h0, left_n) + _dot(btg, xw)

    @pl.when(c == pl.num_programs(2) - 1)
    def _():
        hout_ref[0, 0] = ht_s[...].T.reshape(SSD_HEADS, SSD_HEADDIM, SSD_STATE)


def ssd_scan(xs, bt, cm, dt2, dtt, a, h0):
    bsz, seq, _ = xs.shape
    q = SSD_CHUNK
    nc = seq // q
    cidx = lambda d, c: c + d * (nc - 1 - 2 * c)
    in_specs = [pl.BlockSpec((1, q, SSD_INNER), lambda b, d, c: (b, cidx(d, c), 0)),
                pl.BlockSpec((1, SSD_BC, q), lambda b, d, c: (b, 0, cidx(d, c))),
                pl.BlockSpec((1, q, SSD_BC), lambda b, d, c: (b, cidx(d, c), 0)),
                pl.BlockSpec((1, 1, q, SSD_HEADS), lambda b, d, c: (d, b, cidx(d, c), 0)),
                pl.BlockSpec((1, SSD_HEADS, q), lambda b, d, c: (b, d, cidx(d, c))),
                pl.BlockSpec((1, 1, SSD_HEADS), lambda b, d, c: (d, 0, 0)),
                pl.BlockSpec((1, SSD_HEADS, 1), lambda b, d, c: (d, 0, 0))]
    args = [xs, bt, cm, dt2, dtt, a.reshape(2, 1, SSD_HEADS), a.reshape(2, SSD_HEADS, 1)]
    st_spec = pl.BlockSpec((1, 1, SSD_HEADS, SSD_HEADDIM, SSD_STATE), lambda b, d, c: (b, d, 0, 0, 0))
    if h0 is not None:
        in_specs.append(st_spec)
        args.append(h0)
    return pl.pallas_call(
        functools.partial(_ssd_scan_body, has_h0=h0 is not None),
        grid=(bsz, 2, nc),
        in_specs=in_specs,
        out_specs=[pl.BlockSpec((1, 1, q, SSD_INNER), lambda b, d, c: (d, b, cidx(d, c), 0)), st_spec],
        out_shape=[jax.ShapeDtypeStruct((2, bsz, seq, SSD_INNER), F32),
                   jax.ShapeDtypeStruct((bsz, 2, SSD_HEADS, SSD_HEADDIM, SSD_STATE), F32)],
        scratch_shapes=[pltpu.VMEM((SSD_STATE, SSD_INNER), F32)],
        compiler_params=_cparams("parallel", "arbitrary", "arbitrary"),
        name="ssd_scan",
    )(*args)


def _ssd_weights(w_in, conv_w, conv_b, dt_bias, a_log, d_skip, norm_g, w_out):
    o1, o2 = SSD_INNER, SSD_INNER + SSD_CONV_DIM
    w_dt = w_in[:, o2:]
    operands = (w_in[:, :o1].astype(BF16), w_in[:, o1:o2].astype(BF16),
                w_dt.astype(BF16), dt_bias.reshape(1, -1),
                w_dt.T.astype(BF16), dt_bias.reshape(-1, 1))
    a = -jnp.exp(a_log.astype(F32))
    dsk = jnp.repeat(d_skip, SSD_HEADDIM).reshape(1, SSD_INNER)
    return operands, conv_w, conv_b, a, dsk, norm_g.reshape(1, SSD_INNER), w_out.astype(BF16)


_SSD_PIECES = (_Piece(SSD_INNER), _Piece(SSD_CONV_DIM),
               _Piece(2 * SSD_HEADS, bias=True, act=lambda r, t: _softplus(r)),
               _Piece(2 * SSD_HEADS, transposed=True, bias=True, act=lambda r, t: _softplus(r)))


def ssd_layer(x, g, mod, per_batch, wts, h0):
    operands, conv_w, conv_b, a, dsk, norm_g, w_out = wts
    z, xbc, dt, dtt = inproj(x, g, mod, _SSD_PIECES, operands, per_batch)
    xs, bt, cm = ssd_conv(xbc, conv_w, conv_b)
    dt2 = jnp.stack([dt[..., :SSD_HEADS], dt[..., SSD_HEADS:]], axis=0)
    y, h_new = ssd_scan(xs, bt, cm, dt2, dtt, a, h0)
    xn = outproj("ssd", x, mod, per_batch, w_out, [xs, z], dir_inputs=[y], consts=[dsk, norm_g])
    return xn, h_new


def _ml_qkv_body(x_ref, p_ref, n_ref, cw_ref, cb_ref, wq_ref, wkt_ref, wv_ref, xc_ref, q_ref, kt_ref, v_ref):
    i = pl.program_id(1)
    first = i == 0
    last = i == pl.num_programs(1) - 1
    for h in range(ML_HEADS):
        sl = slice(h * ML_DH, (h + 1) * ML_DH)
        xm = x_ref[0, :, sl]
        xc = _conv_silu(xm, p_ref[0, :, sl], n_ref[0, :, sl], cw_ref[:, sl], cb_ref[:, sl], first, last)
        xc_ref[0, :, sl] = xc
        xc_b = xc.astype(BF16)
        q_ref[0, :, sl] = _dot(xc_b, wq_ref[h]).astype(BF16)
        kt_ref[0, h] = (_dot_nt(wkt_ref[h], xc_b) * (ML_DH ** -0.5)).astype(BF16)
        v_ref[0, :, sl] = _dot(xm.astype(BF16), wv_ref[h]).astype(BF16)


def ml_qkv(xm, conv_w, conv_b, wq, wkt, wv):
    bsz, seq, _ = xm.shape
    tm = TOKEN_TILE
    tok = pl.BlockSpec((1, tm, ML_INNER), lambda b, i: (b, i, 0))
    return pl.pallas_call(
        _ml_qkv_body,
        grid=(bsz, seq // tm),
        in_specs=_halo_specs(tm, ML_INNER, seq) + [_const_spec(conv_w.shape), _const_spec((1, ML_INNER)),
                                                   _const_spec(wq.shape), _const_spec(wkt.shape),
                                                   _const_spec(wv.shape)],
        out_specs=[tok, tok, pl.BlockSpec((1, ML_HEADS, ML_DH, tm), lambda b, i: (b, 0, 0, i)), tok],
        out_shape=[jax.ShapeDtypeStruct((bsz, seq, ML_INNER), F32),
                   jax.ShapeDtypeStruct((bsz, seq, ML_INNER), BF16),
                   jax.ShapeDtypeStruct((bsz, ML_HEADS, ML_DH, seq), BF16),
                   jax.ShapeDtypeStruct((bsz, seq, ML_INNER), BF16)],
        compiler_params=_cparams("parallel", "parallel"),
        name="ml_qkv",
    )(xm, xm, xm, conv_w, conv_b.reshape(1, ML_INNER), wq, wkt, wv)


def _ml_scan_body(*refs, has_state):
    if has_state:
        q_ref, kt_ref, v_ref, g_ref, gt_ref, c0_ref, n0_ref, m0_ref, h_ref, c_ref, n_ref, m_ref = refs
    else:
        q_ref, kt_ref, v_ref, g_ref, gt_ref, h_ref, c_ref, n_ref, m_ref = refs
    d = pl.program_id(1)
    c = pl.program_id(2)
    lc = q_ref.shape[1]
    nh = ML_HEADS

    @pl.when(c == 0)
    def _():
        if has_state:
            c_ref[...] = c0_ref[...]
            n_ref[...] = n0_ref[...]
            m_ref[...] = m0_ref[...]
        else:
            c_ref[...] = jnp.zeros_like(c_ref)
            n_ref[...] = jnp.zeros_like(n_ref)
            m_ref[...] = jnp.zeros_like(m_ref)

    mask, mask_t = _scan_masks(d, lc)
    fwd = d == 0
    gates = g_ref[0, 0]
    gates_t = gt_ref[0]
    lf = gates[:, nh:2 * nh]
    lf_t = gates_t[nh:2 * nh, :]
    cum = jnp.dot(mask.astype(F32), lf, precision=HIGHEST, preferred_element_type=F32)
    cum_t = jnp.dot(lf_t, mask_t.astype(F32), precision=HIGHEST, preferred_element_type=F32)
    tot = jnp.sum(lf, axis=0, keepdims=True)
    for h in range(nh):
        sl = slice(h * ML_DH, (h + 1) * ML_DH)
        bcol = cum[:, h:h + 1]
        brow = cum_t[h:h + 1, :]
        irow = gates_t[h:h + 1, :]
        m_prev = m_ref[0, 0, h:h + 1, 0:1]
        dm = jnp.where(mask, bcol - brow + irow, -jnp.inf)
        m_inter = bcol + m_prev
        m_i = jnp.maximum(jnp.max(dm, axis=-1, keepdims=True), m_inter)
        wmat = jnp.exp(dm - m_i)
        g_inter = jnp.exp(m_inter - m_i)
        qh = q_ref[0, :, sl]
        kth = kt_ref[0, h]
        vh = v_ref[0, :, sl]
        s = _dot(qh, kth) * wmat
        c_old = c_ref[0, 0, h]
        n_old = n_ref[0, 0, h:h + 1, :]
        num = _dot(s.astype(BF16), vh) + g_inter * _dot(qh, c_old.astype(BF16))
        qn = _dot_nt(qh, jnp.broadcast_to(n_old, (SUBLANES, ML_DH)).astype(BF16))[:, 0:1]
        den = jnp.sum(s, axis=-1, keepdims=True) + g_inter * qn
        denom = jnp.maximum(jnp.abs(den), jnp.exp(-m_i))
        h_ref[0, 0, :, sl] = num / denom
        m_new = jnp.where(fwd, m_i[lc - 1:lc, :], m_i[0:1, :])
        tot_h = tot[:, h:h + 1]
        w_row = jnp.exp(tot_h - brow + irow - m_new)
        decay = jnp.exp(tot_h + m_prev - m_new)
        kw = (kth.astype(F32) * w_row).astype(BF16)
        c_ref[0, 0, h] = decay * c_old + _dot(kw, vh)
        w8 = jnp.broadcast_to(w_row, (SUBLANES, lc)).astype(BF16)
        n_ref[0, 0, h:h + 1, :] = decay * n_old + _dot_nt(w8, kth)[0:1, :]
        m_ref[0, 0, h:h + 1, :] = jnp.broadcast_to(m_new, (1, LANES))


def ml_scan(q, kt, v, gates2, gates_t, state):
    bsz, seq, _ = q.shape
    lc = min(ML_CHUNK, seq)
    nc = seq // lc
    cidx = lambda d, c: c + d * (nc - 1 - 2 * c)
    tok = pl.BlockSpec((1, lc, ML_INNER), lambda b, d, c: (b, cidx(d, c), 0))
    in_specs = [tok,
                pl.BlockSpec((1, ML_HEADS, ML_DH, lc), lambda b, d, c: (b, 0, 0, cidx(d, c))),
                tok,
                pl.BlockSpec((1, 1, lc, 2 * ML_HEADS), lambda b, d, c: (d, b, cidx(d, c), 0)),
                pl.BlockSpec((1, 2 * ML_HEADS, lc), lambda b, d, c: (b, d, cidx(d, c)))]
    args = [q, kt, v, gates2, gates_t]
    st_specs = [pl.BlockSpec((1, 1, ML_HEADS, ML_DH, ML_DH), lambda b, d, c: (b, d, 0, 0, 0)),
                pl.BlockSpec((1, 1, ML_HEADS, ML_DH), lambda b, d, c: (b, d, 0, 0)),
                pl.BlockSpec((1, 1, SUBLANES, LANES), lambda b, d, c: (b, d, 0, 0))]
    if state is not None:
        in_specs += st_specs
        args += list(state)
    return pl.pallas_call(
        functools.partial(_ml_scan_body, has_state=state is not None),
        grid=(bsz, 2, nc),
        in_specs=in_specs,
        out_specs=[pl.BlockSpec((1, 1, lc, ML_INNER), lambda b, d, c: (d, b, cidx(d, c), 0))] + st_specs,
        out_shape=[jax.ShapeDtypeStruct((2, bsz, seq, ML_INNER), F32),
                   jax.ShapeDtypeStruct((bsz, 2, ML_HEADS, ML_DH, ML_DH), F32),
                   jax.ShapeDtypeStruct((bsz, 2, ML_HEADS, ML_DH), F32),
                   jax.ShapeDtypeStruct((bsz, 2, SUBLANES, LANES), F32)],
        compiler_params=_cparams("parallel", "arbitrary", "arbitrary"),
        name="ml_scan",
    )(*args)


def _ml_gate_act(r, transposed):
    idx = lax.broadcasted_iota(jnp.int32, r.shape, 0 if transposed else 1)
    return jnp.where((idx // ML_HEADS) % 2 == 1, _log_sigmoid(r), r)


_ML_PIECES = (_Piece(ML_INNER), _Piece(ML_INNER), _Piece(ML_INNER),
              _Piece(4 * ML_HEADS, bias=True, act=_ml_gate_act),
              _Piece(4 * ML_HEADS, transposed=True, bias=True, act=_ml_gate_act))


def _ml_weights(w_in, gate_b, conv_w, conv_b, w_q, w_k, w_v, norm_g, skip, w_out):
    w_g = w_in[:, 3 * ML_INNER:]
    operands = (w_in[:, :ML_INNER].astype(BF16), w_in[:, ML_INNER:2 * ML_INNER].astype(BF16),
                w_in[:, 2 * ML_INNER:3 * ML_INNER].astype(BF16),
                w_g.astype(BF16), gate_b.reshape(1, -1), w_g.T.astype(BF16), gate_b.reshape(-1, 1))
    return (operands, conv_w, conv_b, w_q.astype(BF16), jnp.swapaxes(w_k, 1, 2).astype(BF16), w_v.astype(BF16),
            norm_g.reshape(1, ML_INNER), skip.reshape(1, ML_INNER), w_out.astype(BF16))


def ml_layer(x, g, mod, per_batch, wts, state):
    operands, conv_w, conv_b, wq, wkt, wv, norm_g, skip, w_out = wts
    xm, o_pre, z, gates, gates_t = inproj(x, g, mod, _ML_PIECES, operands, per_batch)
    xc, q, kt, v = ml_qkv(xm, conv_w, conv_b, wq, wkt, wv)
    gates2 = jnp.stack([gates[..., :2 * ML_HEADS], gates[..., 2 * ML_HEADS:]], axis=0)
    if state is not None:
        c0, n0, m0 = state
        m0 = jnp.broadcast_to(jnp.pad(m0, ((0, 0), (0, 0), (0, SUBLANES - ML_HEADS)))[..., None],
                              m0.shape[:2] + (SUBLANES, LANES))
        state = (c0, n0, m0)
    hs, c_new, n_new, m_new = ml_scan(q, kt, v, gates2, gates_t, state)
    xn = outproj("ml", x, mod, per_batch, w_out, [o_pre, xc, z], dir_inputs=[hs], consts=[norm_g, skip])
    return xn, c_new, n_new, m_new[:, :, :ML_HEADS, 0]


def kernel(x_prompt, x_sample, cache_mla_ckv, cache_mla_kpe, state_ssd, state_mlstm_c, state_mlstm_n,
           state_mlstm_m, c, c_ctx, ada_w, ada_b, norm_g, final_norm_g,
           mla_w_in, mla_q_norm_g, mla_kv_norm_g, mla_w_uq, mla_w_uk, mla_w_uv, mla_w_out,
           ssd_w_in, ssd_conv_w, ssd_conv_b, ssd_dt_bias, ssd_a_log, ssd_d, ssd_norm_g, ssd_w_out,
           ml_w_in, ml_gate_b, ml_conv_w, ml_conv_b, ml_w_q, ml_w_k, ml_w_v, ml_norm_g, ml_skip, ml_w_out):
    nb = c.shape[0]
    rows = -(-(nb + 1) // SUBLANES) * SUBLANES
    cond = jnp.zeros((rows, D), F32).at[0].set(c_ctx).at[1:nb + 1].set(c)
    mods = adaln_all(cond, ada_w, ada_b)
    xp, xs = x_prompt, x_sample
    new_ckv, new_kpe, new_ssd, new_c, new_n, new_m = [], [], [], [], [], []
    yp = ys = None
    for i in range(DEPTH):
        kind, j = i % N_MIXERS, i // N_MIXERS
        mod_p = mods[i, 0:1].reshape(1, 1, 3 * D)
        mod_s = mods[i, 1:nb + 1].reshape(nb, 1, 3 * D)
        fg = final_norm_g if i == DEPTH - 1 else None
        if kind == 0:
            wts = _mla_weights(mla_w_in[j], mla_q_norm_g[j], mla_kv_norm_g[j], mla_w_uq[j], mla_w_uk[j], mla_w_uv[j])
            w_out = mla_w_out[j].astype(BF16)
            rp, ckv, kpe = mla_layer(xp, norm_g[i], mod_p, False, wts, w_out, False, final_g=fg)
            rs, _, _ = mla_layer(xs, norm_g[i], mod_s, True, wts, w_out, True,
                                 ctx=(cache_mla_ckv[:, j], cache_mla_kpe[:, j]), final_g=fg)
            new_ckv.append(ckv)
            new_kpe.append(kpe)
        elif kind == 1:
            wts = _ssd_weights(ssd_w_in[j], ssd_conv_w[j], ssd_conv_b[j], ssd_dt_bias[j], ssd_a_log[j], ssd_d[j],
                               ssd_norm_g[j], ssd_w_out[j])
            rp, st = ssd_layer(xp, norm_g[i], mod_p, False, wts, None)
            rs, _ = ssd_layer(xs, norm_g[i], mod_s, True, wts, state_ssd[:, j])
            new_ssd.append(st)
        else:
            wts = _ml_weights(ml_w_in[j], ml_gate_b[j], ml_conv_w[j], ml_conv_b[j], ml_w_q[j], ml_w_k[j], ml_w_v[j],
                              ml_norm_g[j], ml_skip[j], ml_w_out[j])
            rp, cc, nn, mm = ml_layer(xp, norm_g[i], mod_p, False, wts, None)
            rs, _, _, _ = ml_layer(xs, norm_g[i], mod_s, True, wts,
                                   (state_mlstm_c[:, j], state_mlstm_n[:, j], state_mlstm_m[:, j]))
            new_c.append(cc)
            new_n.append(nn)
            new_m.append(mm)
        if fg is not None:
            (xp, yp), (xs, ys) = rp, rs
        else:
            xp, xs = rp, rs
    return (yp, ys, jnp.stack(new_ckv, axis=1), jnp.stack(new_kpe, axis=1), jnp.stack(new_ssd, axis=1),
            jnp.stack(new_c, axis=1), jnp.stack(new_n, axis=1), jnp.stack(new_m, axis=1))
```

```python
import functools
import math

import jax
import jax.numpy as jnp
from jax import lax
from jax.experimental import pallas as pl
from jax.experimental.pallas import tpu as pltpu

F32 = jnp.float32
BF16 = jnp.bfloat16
HIGHEST = lax.Precision.HIGHEST

D = 1024
DEPTH = 4
GRID_W = 64
N_MIXERS = 3
NORM_EPS = 1e-6
CONV_W = 4

MLA_HEADS = 16
MLA_NOPE = 64
MLA_ROPE = 32
MLA_V = 64
MLA_Q_RANK = 384
MLA_KV_RANK = 256
MLA_SCALE = (MLA_NOPE + MLA_ROPE) ** -0.5
LOG2_E = math.log2(math.e)
ROPE_BASE = 10000.0

SSD_INNER = 2 * D
SSD_HEADDIM = 64
SSD_HEADS = SSD_INNER // SSD_HEADDIM
SSD_STATE = 128
SSD_GROUPS = 8
SSD_BC = SSD_GROUPS * SSD_STATE
SSD_CONV_DIM = SSD_INNER + 2 * SSD_BC
SSD_CHUNK = 128

ML_INNER = 2 * D
ML_HEADS = 4
ML_DH = ML_INNER // ML_HEADS

LANES = 128
SUBLANES = 8
VMEM_LIMIT = 56 * 1024 * 1024

TOKEN_TILE = 256
HALO = SUBLANES
ML_CHUNK = 256
ATTN_TQ = 512
ATTN_TK = 1536


def _cparams(*sem):
    return pltpu.CompilerParams(dimension_semantics=sem, vmem_limit_bytes=VMEM_LIMIT)


def _silu(v):
    return v * jax.nn.sigmoid(v)


def _softplus(v):
    return jnp.maximum(v, 0.0) + jnp.log(1.0 + jnp.exp(-jnp.abs(v)))


def _log_sigmoid(v):
    return jnp.minimum(v, 0.0) - jnp.log(1.0 + jnp.exp(-jnp.abs(v)))


def _modulate(x, g, mod):
    ms = jnp.mean(x * x, axis=-1, keepdims=True)
    y = x * lax.rsqrt(ms + NORM_EPS) * g
    return y * (1.0 + mod[:, D:2 * D]) + mod[:, 0:D]


def _dot(a, b):
    return jnp.dot(a, b, preferred_element_type=F32)


def _dot_nt(a, b):
    return lax.dot_general(a, b, (((1,), (1,)), ((), ())), preferred_element_type=F32)


def _const_spec(shape):
    nd = len(shape)
    return pl.BlockSpec(shape, lambda *_: (0,) * nd)


def _mod_spec(per_batch):
    if per_batch:
        return pl.BlockSpec((1, 1, 3 * D), lambda b, i: (b, 0, 0))
    return pl.BlockSpec((1, 1, 3 * D), lambda b, i: (0, 0, 0))


def _adaln_body(cond_ref, w_ref, b_ref, o_ref):
    a = _silu(cond_ref[...]).astype(BF16)
    o_ref[0] = _dot(a, w_ref[0].astype(BF16)) + b_ref[0]


def adaln_all(cond, ada_w, ada_b):
    r = cond.shape[0]
    return pl.pallas_call(
        _adaln_body,
        grid=(DEPTH, 3),
        in_specs=[pl.BlockSpec((r, D), lambda i, j: (0, 0)),
                  pl.BlockSpec((1, D, D), lambda i, j: (i, 0, j)),
                  pl.BlockSpec((1, 1, D), lambda i, j: (i, 0, j))],
        out_specs=pl.BlockSpec((1, r, D), lambda i, j: (i, 0, j)),
        out_shape=jax.ShapeDtypeStruct((DEPTH, r, 3 * D), F32),
        compiler_params=_cparams("arbitrary", "arbitrary"),
        name="adaln",
    )(cond, ada_w, ada_b.reshape(DEPTH, 1, 3 * D))


class _Piece:
    def __init__(self, width, transposed=False, bias=False, act=None):
        self.width, self.transposed, self.bias, self.act = width, transposed, bias, act


def _inproj_body(x_ref, g_ref, mod_ref, *rest, pieces):
    n_in = sum(2 if p.bias else 1 for p in pieces)
    ins, outs = rest[:n_in], rest[n_in:]
    h = _modulate(x_ref[0], g_ref[...], mod_ref[0]).astype(BF16)
    k = 0
    for p, o_ref in zip(pieces, outs):
        w_ref = ins[k]
        k += 1
        b_ref = None
        if p.bias:
            b_ref = ins[k]
            k += 1
        if p.transposed:
            r = _dot_nt(w_ref[...], h)
            if b_ref is not None:
                r = r + b_ref[...]
            if p.act is not None:
                r = p.act(r, True)
            o_ref[0] = r.astype(o_ref.dtype)
        else:
            for c0 in range(0, p.width, 1024):
                c1 = min(p.width, c0 + 1024)
                r = _dot(h, w_ref[:, c0:c1])
                if b_ref is not None:
                    r = r + b_ref[:, c0:c1]
                if p.act is not None:
                    r = p.act(r, False)
                o_ref[0, :, c0:c1] = r.astype(o_ref.dtype)


def inproj(x, g, mod, pieces, operands, per_batch):
    bsz, seq, _ = x.shape
    tm = TOKEN_TILE
    in_specs = [pl.BlockSpec((1, tm, D), lambda b, i: (b, i, 0)), _const_spec((1, D)), _mod_spec(per_batch)]
    in_specs += [_const_spec(op.shape) for op in operands]
    out_specs, out_shapes = [], []
    for p in pieces:
        if p.transposed:
            out_specs.append(pl.BlockSpec((1, p.width, tm), lambda b, i: (b, 0, i)))
            out_shapes.append(jax.ShapeDtypeStruct((bsz, p.width, seq), F32))
        else:
            out_specs.append(pl.BlockSpec((1, tm, p.width), lambda b, i: (b, i, 0)))
            out_shapes.append(jax.ShapeDtypeStruct((bsz, seq, p.width), F32))
    return pl.pallas_call(
        functools.partial(_inproj_body, pieces=pieces),
        grid=(bsz, seq // tm),
        in_specs=in_specs, out_specs=out_specs, out_shape=out_shapes,
        compiler_params=_cparams("parallel", "parallel"),
        name="inproj",
    )(x, g.reshape(1, D), mod, *operands)


def _halo_specs(tm, width, seq):
    r = tm // HALO
    last = seq // HALO - 1
    return [pl.BlockSpec((1, tm, width), lambda b, i: (b, i, 0)),
            pl.BlockSpec((1, HALO, width), lambda b, i: (b, jnp.maximum(i * r - 1, 0), 0)),
            pl.BlockSpec((1, HALO, width), lambda b, i: (b, jnp.minimum((i + 1) * r, last), 0))]


def _conv_silu(main, prev8, next8, w, b, first, last):
    tm = main.shape[0]
    row = lax.broadcasted_iota(jnp.int32, main.shape, 0)
    before = jnp.where(first, 0.0, prev8[HALO - 1:HALO, :])
    after0 = jnp.where(last, 0.0, next8[0:1, :])
    after1 = jnp.where(last, 0.0, next8[1:2, :])
    xm1 = jnp.where(row == 0, before, pltpu.roll(main, 1, 0))
    xp1 = jnp.where(row == tm - 1, after0, pltpu.roll(main, tm - 1, 0))
    xp2 = jnp.where(row == tm - 2, after0, jnp.where(row == tm - 1, after1, pltpu.roll(main, tm - 2, 0)))
    y = w[0:1, :] * xm1 + w[1:2, :] * main + w[2:3, :] * xp1 + w[3:4, :] * xp2 + b
    return _silu(y)


def _rms(v, g):
    return v * lax.rsqrt(jnp.mean(v * v, axis=-1, keepdims=True) + NORM_EPS) * g


def _outproj_body(*refs, kind, final_norm):
    if kind == "mla":
        o_ref, z_ref, x_ref, mod_ref, w_ref = refs[:5]
        rest = refs[5:]
        pre = o_ref[0] * _silu(z_ref[0])
    elif kind == "ssd":
        yf_ref, yb_ref, xs_ref, z_ref, x_ref, mod_ref, w_ref, dsk_ref, ng_ref = refs[:9]
        rest = refs[9:]
        y = yf_ref[0, 0] + yb_ref[0, 0] + dsk_ref[...] * xs_ref[0]
        pre = _rms(y * _silu(z_ref[0]), ng_ref[...])
    else:
        hf_ref, hb_ref, op_ref, xc_ref, z_ref, x_ref, mod_ref, w_ref, ng_ref, sk_ref = refs[:10]
        rest = refs[10:]
        hh = (hf_ref[0, 0] + hb_ref[0, 0]) * jax.nn.sigmoid(op_ref[0])
        parts = []
        for h in range(ML_HEADS):
            v = hh[:, h * ML_DH:(h + 1) * ML_DH]
            mu = jnp.mean(v, axis=-1, keepdims=True)
            vc = v - mu
            var = jnp.mean(vc * vc, axis=-1, keepdims=True)
            parts.append(vc * lax.rsqrt(var + NORM_EPS))
        hn = jnp.concatenate(parts, axis=-1) * ng_ref[...]
        pre = (hn + sk_ref[...] * xc_ref[0]) * _silu(z_ref[0])
    out = _dot(pre.astype(BF16), w_ref[...])
    xn = x_ref[0] + mod_ref[0][:, 2 * D:3 * D] * out
    if final_norm:
        fg_ref, xo_ref, yo_ref = rest
        xo_ref[0] = xn
        yo_ref[0] = _rms(xn, fg_ref[...])
    else:
        (xo_ref,) = rest
        xo_ref[0] = xn


def outproj(kind, x, mod, per_batch, w_out, tok_inputs, dir_inputs=(), consts=(), final_g=None):
    bsz, seq, _ = x.shape
    tm = TOKEN_TILE
    args, in_specs = [], []
    for a in dir_inputs:
        wd = a.shape[-1]
        for d in range(2):
            args.append(a)
            in_specs.append(pl.BlockSpec((1, 1, tm, wd), functools.partial(lambda b, i, d: (d, b, i, 0), d=d)))
    for a in tok_inputs:
        args.append(a)
        in_specs.append(pl.BlockSpec((1, tm, a.shape[-1]), lambda b, i: (b, i, 0)))
    args += [x, mod, w_out]
    in_specs += [pl.BlockSpec((1, tm, D), lambda b, i: (b, i, 0)), _mod_spec(per_batch), _const_spec(w_out.shape)]
    for a in consts:
        args.append(a)
        in_specs.append(_const_spec(a.shape))
    tok_spec = pl.BlockSpec((1, tm, D), lambda b, i: (b, i, 0))
    shp = jax.ShapeDtypeStruct((bsz, seq, D), F32)
    if final_g is not None:
        args.append(final_g.reshape(1, D))
        in_specs.append(_const_spec((1, D)))
        out_specs, out_shape = [tok_spec, tok_spec], [shp, shp]
    else:
        out_specs, out_shape = [tok_spec], [shp]
    res = pl.pallas_call(
        functools.partial(_outproj_body, kind=kind, final_norm=final_g is not None),
        grid=(bsz, seq // tm),
        in_specs=in_specs, out_specs=out_specs, out_shape=out_shape,
        compiler_params=_cparams("parallel", "parallel"),
        name="outproj_" + kind,
    )(*args)
    return res if final_g is not None else res[0]


def _mla_pre_body(x_ref, g_ref, mod_ref, wcq_ref, wckv_ref, wkpe_ref, wkpes_ref, wz_ref, qg_ref, kvg_ref,
                  wuq1_ref, wuq2_ref, cq_ref, sq_ref, ck_ref, sk_ref,
                  q_ref, ckv_ref, kraw_ref, krope_ref, z_ref):
    h = _modulate(x_ref[0], g_ref[...], mod_ref[0]).astype(BF16)
    z_ref[0] = _dot(h, wz_ref[...])
    ckv_ref[0] = _rms(_dot(h, wckv_ref[...]), kvg_ref[...])
    kraw = _dot(h, wkpe_ref[...])
    kraw_ref[0] = kraw
    krope_ref[0] = kraw * ck_ref[...] + _dot(h, wkpes_ref[...]) * sk_ref[...]
    cq = _rms(_dot(h, wcq_ref[...]), qg_ref[...]).astype(BF16)
    cosq = cq_ref[...]
    sinq = sq_ref[...]
    for c0 in range(0, MLA_HEADS * LANES, 512):
        a = _dot(cq, wuq1_ref[:, c0:c0 + 512])
        b = _dot(cq, wuq2_ref[:, c0:c0 + 512])
        for s in range(0, 512, LANES):
            q_ref[0, :, c0 + s:c0 + s + LANES] = (
                a[:, s:s + LANES] * cosq + b[:, s:s + LANES] * sinq).astype(BF16)


def _head_slabs(w, off=0):
    k, nh, dh = w.shape
    return jnp.pad(w, ((0, 0), (0, 0), (off, LANES - off - dh))).reshape(k, nh * LANES)


def _swap_halves(w):
    half = w.shape[-1] // 2
    return jnp.concatenate([w[..., half:], w[..., :half]], axis=-1)


def _rope_tables(seq, use_rope):
    zeros32 = jnp.zeros((seq, LANES - MLA_NOPE - MLA_ROPE), F32)
    if use_rope:
        rows = seq // GRID_W
        row = jnp.repeat(jnp.arange(rows), GRID_W).astype(F32)
        col = jnp.tile(jnp.arange(GRID_W), rows).astype(F32)
        half = MLA_ROPE // 2
        inv = 1.0 / (ROPE_BASE ** (jnp.arange(0, half, 2, dtype=F32) / half))
        ang = jnp.concatenate([row[:, None] * inv, col[:, None] * inv], axis=-1)
        cos, sin = jnp.cos(ang), jnp.sin(ang)
    else:
        cos = jnp.ones((seq, MLA_ROPE // 2), F32)
        sin = jnp.zeros((seq, MLA_ROPE // 2), F32)
    ones64 = jnp.ones((seq, MLA_NOPE), F32)
    zeros64 = jnp.zeros((seq, MLA_NOPE), F32)
    cos_k = jnp.concatenate([zeros64, cos, cos, zeros32], axis=-1)
    sin_k = jnp.concatenate([zeros64, -sin, sin, zeros32], axis=-1)
    q_scale = MLA_SCALE * LOG2_E
    cos_q = jnp.concatenate([ones64, cos, cos, zeros32], axis=-1) * q_scale
    sin_q = sin_k * q_scale
    return cos_q, sin_q, cos_k, sin_k


def mla_pre(x, g, mod, per_batch, wts, tables):
    bsz, seq, _ = x.shape
    tm = TOKEN_TILE
    tok = lambda w, dt: (pl.BlockSpec((1, tm, w), lambda b, i: (b, i, 0)), jax.ShapeDtypeStruct((bsz, seq, w), dt))
    outs = [tok(MLA_HEADS * LANES, BF16), tok(MLA_KV_RANK, F32), tok(LANES, F32), tok(LANES, F32), tok(D, F32)]
    tab_spec = pl.BlockSpec((tm, LANES), lambda b, i: (i, 0))
    in_specs = ([pl.BlockSpec((1, tm, D), lambda b, i: (b, i, 0)), _const_spec((1, D)), _mod_spec(per_batch)]
                + [_const_spec(w.shape) for w in wts] + [tab_spec] * 4)
    return pl.pallas_call(
        _mla_pre_body,
        grid=(bsz, seq // tm),
        in_specs=in_specs,
        out_specs=[o[0] for o in outs], out_shape=[o[1] for o in outs],
        compiler_params=_cparams("parallel", "parallel"),
        name="mla_pre",
    )(x, g.reshape(1, D), mod, *wts, *tables)


def _kv_expand_body(ckv_ref, kpe_ref, wkt_ref, wv_ref, kt_ref, v_ref):
    c = ckv_ref[0].astype(BF16)
    v_ref[0] = _dot(c, wv_ref[...]).astype(BF16)
    kpe_t = kpe_ref[0].T
    for c0 in range(0, MLA_HEADS * LANES, 512):
        a = _dot_nt(wkt_ref[c0:c0 + 512, :], c)
        for s in range(0, 512, LANES):
            kt_ref[0, 0, c0 + s:c0 + s + LANES, :] = (a[s:s + LANES, :] + kpe_t).astype(BF16)


def kv_expand(ckv, kpe_slab, wkt_aug, wv, tk):
    bsz, seq, _ = ckv.shape
    return pl.pallas_call(
        _kv_expand_body,
        grid=(bsz, seq // tk),
        in_specs=[pl.BlockSpec((1, tk, MLA_KV_RANK), lambda b, i: (b, i, 0)),
                  pl.BlockSpec((1, tk, LANES), lambda b, i: (b, i, 0)),
                  _const_spec(wkt_aug.shape), _const_spec(wv.shape)],
        out_specs=[pl.BlockSpec((1, 1, MLA_HEADS * LANES, tk), lambda b, i: (b, i, 0, 0)),
                   pl.BlockSpec((1, tk, MLA_HEADS * MLA_V), lambda b, i: (b, i, 0))],
        out_shape=[jax.ShapeDtypeStruct((bsz, seq // tk, MLA_HEADS * LANES, tk), BF16),
                   jax.ShapeDtypeStruct((bsz, seq, MLA_HEADS * MLA_V), BF16)],
        compiler_params=_cparams("parallel", "parallel"),
        name="mla_kv_expand",
    )(ckv, kpe_slab, wkt_aug, wv)


ATTN_ROWS = 16


def _attn_body(q_ref, kt_ref, v_ref, o_ref, s_ref, p_ref, m_ref, l_ref, a_ref, acc_ref, *, nk):
    tq = q_ref.shape[1]
    tk = kt_ref.shape[3]
    rb = ATTN_ROWS
    ncol = tk // LANES
    left = lax.broadcasted_iota(jnp.int32, (tq, LANES), 1) < MLA_V

    def scores(kc):
        for hh in range(2):
            s_ref[kc % 2, hh] = _dot(q_ref[0, :, hh * LANES:(hh + 1) * LANES],
                                     kt_ref[0, kc, hh * LANES:(hh + 1) * LANES, :])

    def softmax(kc):
        sl = kc % 2
        for hh in range(2):
            for r in range(0, tq, rb):
                mx = s_ref[sl, hh, r:r + rb, 0:LANES]
                for cc in range(1, ncol):
                    mx = jnp.maximum(mx, s_ref[sl, hh, r:r + rb, cc * LANES:(cc + 1) * LANES])
                mx = jnp.broadcast_to(jnp.max(mx, axis=-1, keepdims=True), (rb, LANES))
                if kc > 0:
                    m_old = m_ref[hh, r:r + rb, :]
                    mx = jnp.maximum(m_old, mx)
                    a_ref[hh, r:r + rb, :] = jnp.exp2(m_old - mx)
                m_ref[hh, r:r + rb, :] = mx
            for r in range(0, tq, rb):
                m_new = m_ref[hh, r:r + rb, :]
                tot = None
                for cc in range(ncol):
                    p = jnp.exp2(s_ref[sl, hh, r:r + rb, cc * LANES:(cc + 1) * LANES] - m_new)
                    p_ref[hh, r:r + rb, cc * LANES:(cc + 1) * LANES] = p.astype(BF16)
                    tot = p if tot is None else tot + p
                tot = jnp.broadcast_to(jnp.sum(tot, axis=-1, keepdims=True), (rb, LANES))
                if kc > 0:
                    tot = a_ref[hh, r:r + rb, :] * l_ref[hh, r:r + rb, :] + tot
                l_ref[hh, r:r + rb, :] = tot

    def values(kc):
        vv = v_ref[0, kc * tk:(kc + 1) * tk, :]
        pv = jnp.where(left, _dot(p_ref[0], vv), _dot(p_ref[1], vv))
        if kc > 0:
            pv = acc_ref[...] * jnp.where(left, a_ref[0], a_ref[1]) + pv
        acc_ref[...] = pv

    scores(0)
    for kc in range(nk):
        if kc + 1 < nk:
            scores(kc + 1)
        softmax(kc)
        values(kc)
    o_ref[0] = acc_ref[...] / jnp.where(left, l_ref[0], l_ref[1])


def mla_attention(q_aug, kt, v):
    bsz, seq, _ = q_aug.shape
    _, nk, _, tk = kt.shape
    lk = nk * tk
    tq = min(ATTN_TQ, seq)
    stat = pltpu.VMEM((2, tq, LANES), F32)
    return pl.pallas_call(
        functools.partial(_attn_body, nk=nk),
        grid=(bsz, MLA_HEADS // 2, seq // tq),
        in_specs=[pl.BlockSpec((1, tq, 2 * LANES), lambda b, h, i: (b, i, h)),
                  pl.BlockSpec((1, nk, 2 * LANES, tk), lambda b, h, i: (b, 0, h, 0)),
                  pl.BlockSpec((1, lk, LANES), lambda b, h, i: (b, 0, h))],
        out_specs=pl.BlockSpec((1, tq, LANES), lambda b, h, i: (b, i, h)),
        out_shape=jax.ShapeDtypeStruct((bsz, seq, MLA_HEADS * MLA_V), F32),
        scratch_shapes=[pltpu.VMEM((2, 2, tq, tk), F32), pltpu.VMEM((2, tq, tk), BF16), stat, stat, stat,
                        pltpu.VMEM((tq, LANES), F32)],
        compiler_params=_cparams("parallel", "parallel", "arbitrary"),
        name="mla_attention",
    )(q_aug, kt, v)


def _mla_weights(w_in, q_norm_g, kv_norm_g, w_uq, w_uk, w_uv):
    o1, o2, o3 = MLA_Q_RANK, MLA_Q_RANK + MLA_KV_RANK, MLA_Q_RANK + MLA_KV_RANK + MLA_ROPE
    w_kpe = w_in[:, o2:o3]
    pad_k = lambda w: jnp.pad(w, ((0, 0), (MLA_NOPE, LANES - MLA_NOPE - MLA_ROPE)))
    uq_pe = w_uq[..., MLA_NOPE:]
    wuq1 = _head_slabs(w_uq)
    wuq2 = _head_slabs(_swap_halves(uq_pe), off=MLA_NOPE)
    pre = (w_in[:, :o1].astype(BF16), w_in[:, o1:o2].astype(BF16), pad_k(w_kpe).astype(BF16),
           pad_k(_swap_halves(w_kpe)).astype(BF16), w_in[:, o3:].astype(BF16),
           q_norm_g.reshape(1, -1), kv_norm_g.reshape(1, -1), wuq1.astype(BF16), wuq2.astype(BF16))
    wkt_aug = _head_slabs(w_uk).T.astype(BF16)
    wv = w_uv.reshape(MLA_KV_RANK, MLA_HEADS * MLA_V).astype(BF16)
    return pre, wkt_aug, wv


def _key_chunk(lk):
    tk = min(ATTN_TK, lk)
    while lk % tk:
        tk -= LANES
    return tk


def mla_layer(x, g, mod, per_batch, wts, w_out, use_rope, ctx=None, final_g=None):
    pre_w, wkt_aug, wv = wts
    seq = x.shape[1]
    q_aug, ckv, kraw, krope, z = mla_pre(x, g, mod, per_batch, pre_w, _rope_tables(seq, use_rope))
    ckv_all, kpe_all = ckv, krope
    if ctx is not None:
        ctx_ckv, ctx_kpe = ctx
        ckv_all = jnp.concatenate([ckv, ctx_ckv], axis=1)
        kpe_all = jnp.concatenate(
            [krope, jnp.pad(ctx_kpe, ((0, 0), (0, 0), (MLA_NOPE, LANES - MLA_NOPE - MLA_ROPE)))], axis=1)
    kt, v = kv_expand(ckv_all, kpe_all, wkt_aug, wv, _key_chunk(ckv_all.shape[1]))
    o = mla_attention(q_aug, kt, v)
    res = outproj("mla", x, mod, per_batch, w_out, [o, z], final_g=final_g)
    return res, ckv, kraw[..., MLA_NOPE:MLA_NOPE + MLA_ROPE]


def _ssd_conv_body(x_ref, p_ref, n_ref, w_ref, b_ref, xs_ref, bt_ref, c_ref):
    i = pl.program_id(1)
    first = i == 0
    last = i == pl.num_programs(1) - 1
    cw = 512
    for c0 in range(0, SSD_CONV_DIM, cw):
        y = _conv_silu(x_ref[0, :, c0:c0 + cw], p_ref[0, :, c0:c0 + cw], n_ref[0, :, c0:c0 + cw],
                       w_ref[:, c0:c0 + cw], b_ref[:, c0:c0 + cw], first, last)
        if c0 < SSD_INNER:
            xs_ref[0, :, c0:c0 + cw] = y
        elif c0 < SSD_INNER + SSD_BC:
            o = c0 - SSD_INNER
            bt_ref[0, o:o + cw, :] = y.T.astype(BF16)
        else:
            o = c0 - SSD_INNER - SSD_BC
            c_ref[0, :, o:o + cw] = y.astype(BF16)


def ssd_conv(xbc, conv_w, conv_b):
    bsz, seq, _ = xbc.shape
    tm = TOKEN_TILE
    return pl.pallas_call(
        _ssd_conv_body,
        grid=(bsz, seq // tm),
        in_specs=_halo_specs(tm, SSD_CONV_DIM, seq) + [_const_spec(conv_w.shape), _const_spec((1, SSD_CONV_DIM))],
        out_specs=[pl.BlockSpec((1, tm, SSD_INNER), lambda b, i: (b, i, 0)),
                   pl.BlockSpec((1, SSD_BC, tm), lambda b, i: (b, 0, i)),
                   pl.BlockSpec((1, tm, SSD_BC), lambda b, i: (b, i, 0))],
        out_shape=[jax.ShapeDtypeStruct((bsz, seq, SSD_INNER), F32),
                   jax.ShapeDtypeStruct((bsz, SSD_BC, seq), BF16),
                   jax.ShapeDtypeStruct((bsz, seq, SSD_BC), BF16)],
        compiler_params=_cparams("parallel", "parallel"),
        name="ssd_conv",
    )(xbc, xbc, xbc, conv_w, conv_b.reshape(1, SSD_CONV_DIM))


def _scan_masks(d, n):
    ri = lax.broadcasted_iota(jnp.int32, (n, n), 0)
    ci = lax.broadcasted_iota(jnp.int32, (n, n), 1)
    diff = (ri - ci) * (1 - 2 * d)
    return diff >= 0, diff <= 0


def _ssd_scan_body(*refs, has_h0):
    if has_h0:
        xs_ref, bt_ref, c_ref, dt_ref, dtt_ref, a_ref, at_ref, h0_ref, y_ref, hout_ref, ht_s = refs
    else:
        xs_ref, bt_ref, c_ref, dt_ref, dtt_ref, a_ref, at_ref, y_ref, hout_ref, ht_s = refs
    d = pl.program_id(1)
    c = pl.program_id(2)
    q = SSD_CHUNK
    hp_n = SSD_HEADS * SSD_HEADDIM

    @pl.when(c == 0)
    def _():
        if has_h0:
            ht_s[...] = h0_ref[0, 0].reshape(hp_n, SSD_STATE).T
        else:
            ht_s[...] = jnp.zeros_like(ht_s)

    mask, mask_t = _scan_masks(d, q)
    dt = dt_ref[0, 0]
    dtt = dtt_ref[0]
    da = dt * a_ref[0]
    dat = dtt * at_ref[0]
    cum = jnp.dot(mask.astype(F32), da, precision=HIGHEST, preferred_element_type=F32)
    cum_t = jnp.dot(dat, mask_t.astype(F32), precision=HIGHEST, preferred_element_type=F32)
    tot = jnp.sum(da, axis=0, keepdims=True)
    exp_cum = jnp.exp(cum)
    w_end = jnp.exp(tot - cum) * dt
    decay = jnp.exp(tot)
    left = lax.broadcasted_iota(jnp.int32, (q, LANES), 1) < SSD_HEADDIM
    left_n = lax.broadcasted_iota(jnp.int32, (SSD_STATE, LANES), 1) < SSD_HEADDIM
    gw = (SSD_HEADS // SSD_GROUPS) * SSD_HEADDIM

    def pick(arr, h, cond):
        return jnp.where(cond, arr[:, h:h + 1], arr[:, h + 1:h + 2])

    for g in range(SSD_GROUPS):
        cg = c_ref[0, :, g * SSD_STATE:(g + 1) * SSD_STATE]
        btg = bt_ref[0, g * SSD_STATE:(g + 1) * SSD_STATE, :]
        cb = _dot(cg, btg)
        y_state = _dot(cg, ht_s[:, g * gw:(g + 1) * gw].astype(BF16))
        for pr in range(gw // LANES):
            lo = g * gw + pr * LANES
            h0 = lo // SSD_HEADDIM
            xs = xs_ref[0, :, lo:lo + LANES]
            xs_b = xs.astype(BF16)
            ys = []
            for h in (h0, h0 + 1):
                seg = cum[:, h:h + 1] - cum_t[h:h + 1, :]
                lm = jnp.exp(jnp.where(mask, seg, -jnp.inf))
                mm = (cb * lm * dtt[h:h + 1, :]).astype(BF16)
                ys.append(_dot(mm, xs_b))
            y_ref[0, 0, :, lo:lo + LANES] = (jnp.where(left, ys[0], ys[1])
                                             + y_state[:, pr * LANES:(pr + 1) * LANES] * pick(exp_cum, h0, left))
            xw = (xs * pick(w_end, h0, left)).astype(BF16)
            ht_s[:, lo:lo + LANES] = ht_s[:, lo:lo + LANES] * pick(decay, h0, left_n) + _dot(btg, xw)

    @pl.when(c == pl.num_programs(2) - 1)
    def _():
        hout_ref[0, 0] = ht_s[...].T.reshape(SSD_HEADS, SSD_HEADDIM, SSD_STATE)


def ssd_scan(xs, bt, cm, dt2, dtt, a, h0):
    bsz, seq, _ = xs.shape
    q = SSD_CHUNK
    nc = seq // q
    cidx = lambda d, c: c + d * (nc - 1 - 2 * c)
    in_specs = [pl.BlockSpec((1, q, SSD_INNER), lambda b, d, c: (b, cidx(d, c), 0)),
                pl.BlockSpec((1, SSD_BC, q), lambda b, d, c: (b, 0, cidx(d, c))),
                pl.BlockSpec((1, q, SSD_BC), lambda b, d, c: (b, cidx(d, c), 0)),
                pl.BlockSpec((1, 1, q, SSD_HEADS), lambda b, d, c: (d, b, cidx(d, c), 0)),
                pl.BlockSpec((1, SSD_HEADS, q), lambda b, d, c: (b, d, cidx(d, c))),
                pl.BlockSpec((1, 1, SSD_HEADS), lambda b, d, c: (d, 0, 0)),
                pl.BlockSpec((1, SSD_HEADS, 1), lambda b, d, c: (d, 0, 0))]
    args = [xs, bt, cm, dt2, dtt, a.reshape(2, 1, SSD_HEADS), a.reshape(2, SSD_HEADS, 1)]
    st_spec = pl.BlockSpec((1, 1, SSD_HEADS, SSD_HEADDIM, SSD_STATE), lambda b, d, c: (b, d, 0, 0, 0))
    if h0 is not None:
        in_specs.append(st_spec)
        args.append(h0)
    return pl.pallas_call(
        functools.partial(_ssd_scan_body, has_h0=h0 is not None),
        grid=(bsz, 2, nc),
        in_specs=in_specs,
        out_specs=[pl.BlockSpec((1, 1, q, SSD_INNER), lambda b, d, c: (d, b, cidx(d, c), 0)), st_spec],
        out_shape=[jax.ShapeDtypeStruct((2, bsz, seq, SSD_INNER), F32),
                   jax.ShapeDtypeStruct((bsz, 2, SSD_HEADS, SSD_HEADDIM, SSD_STATE), F32)],
        scratch_shapes=[pltpu.VMEM((SSD_STATE, SSD_INNER), F32)],
        compiler_params=_cparams("parallel", "arbitrary", "arbitrary"),
        name="ssd_scan",
    )(*args)


def _ssd_weights(w_in, conv_w, conv_b, dt_bias, a_log, d_skip, norm_g, w_out):
    o1, o2 = SSD_INNER, SSD_INNER + SSD_CONV_DIM
    w_dt = w_in[:, o2:]
    operands = (w_in[:, :o1].astype(BF16), w_in[:, o1:o2].astype(BF16),
                w_dt.astype(BF16), dt_bias.reshape(1, -1),
                w_dt.T.astype(BF16), dt_bias.reshape(-1, 1))
    a = -jnp.exp(a_log.astype(F32))
    dsk = jnp.repeat(d_skip, SSD_HEADDIM).reshape(1, SSD_INNER)
    return operands, conv_w, conv_b, a, dsk, norm_g.reshape(1, SSD_INNER), w_out.astype(BF16)


_SSD_PIECES = (_Piece(SSD_INNER), _Piece(SSD_CONV_DIM),
               _Piece(2 * SSD_HEADS, bias=True, act=lambda r, t: _softplus(r)),
               _Piece(2 * SSD_HEADS, transposed=True, bias=True, act=lambda r, t: _softplus(r)))


def ssd_layer(x, g, mod, per_batch, wts, h0):
    operands, conv_w, conv_b, a, dsk, norm_g, w_out = wts
    z, xbc, dt, dtt = inproj(x, g, mod, _SSD_PIECES, operands, per_batch)
    xs, bt, cm = ssd_conv(xbc, conv_w, conv_b)
    dt2 = jnp.stack([dt[..., :SSD_HEADS], dt[..., SSD_HEADS:]], axis=0)
    y, h_new = ssd_scan(xs, bt, cm, dt2, dtt, a, h0)
    xn = outproj("ssd", x, mod, per_batch, w_out, [xs, z], dir_inputs=[y], consts=[dsk, norm_g])
    return xn, h_new


def _ml_qkv_body(x_ref, p_ref, n_ref, cw_ref, cb_ref, wq_ref, wkt_ref, wv_ref, xc_ref, q_ref, kt_ref, v_ref):
    i = pl.program_id(1)
    first = i == 0
    last = i == pl.num_programs(1) - 1
    for h in range(ML_HEADS):
        sl = slice(h * ML_DH, (h + 1) * ML_DH)
        xm = x_ref[0, :, sl]
        xc = _conv_silu(xm, p_ref[0, :, sl], n_ref[0, :, sl], cw_ref[:, sl], cb_ref[:, sl], first, last)
        xc_ref[0, :, sl] = xc
        xc_b = xc.astype(BF16)
        q_ref[0, :, sl] = _dot(xc_b, wq_ref[h]).astype(BF16)
        kt_ref[0, h] = (_dot_nt(wkt_ref[h], xc_b) * (ML_DH ** -0.5)).astype(BF16)
        v_ref[0, :, sl] = _dot(xm.astype(BF16), wv_ref[h]).astype(BF16)


def ml_qkv(xm, conv_w, conv_b, wq, wkt, wv):
    bsz, seq, _ = xm.shape
    tm = TOKEN_TILE
    tok = pl.BlockSpec((1, tm, ML_INNER), lambda b, i: (b, i, 0))
    return pl.pallas_call(
        _ml_qkv_body,
        grid=(bsz, seq // tm),
        in_specs=_halo_specs(tm, ML_INNER, seq) + [_const_spec(conv_w.shape), _const_spec((1, ML_INNER)),
                                                   _const_spec(wq.shape), _const_spec(wkt.shape),
                                                   _const_spec(wv.shape)],
        out_specs=[tok, tok, pl.BlockSpec((1, ML_HEADS, ML_DH, tm), lambda b, i: (b, 0, 0, i)), tok],
        out_shape=[jax.ShapeDtypeStruct((bsz, seq, ML_INNER), F32),
                   jax.ShapeDtypeStruct((bsz, seq, ML_INNER), BF16),
                   jax.ShapeDtypeStruct((bsz, ML_HEADS, ML_DH, seq), BF16),
                   jax.ShapeDtypeStruct((bsz, seq, ML_INNER), BF16)],
        compiler_params=_cparams("parallel", "parallel"),
        name="ml_qkv",
    )(xm, xm, xm, conv_w, conv_b.reshape(1, ML_INNER), wq, wkt, wv)


def _ml_scan_body(*refs, has_state):
    if has_state:
        q_ref, kt_ref, v_ref, g_ref, gt_ref, c0_ref, n0_ref, m0_ref, h_ref, c_ref, n_ref, m_ref = refs
    else:
        q_ref, kt_ref, v_ref, g_ref, gt_ref, h_ref, c_ref, n_ref, m_ref = refs
    d = pl.program_id(1)
    c = pl.program_id(2)
    lc = q_ref.shape[1]
    nh = ML_HEADS

    @pl.when(c == 0)
    def _():
        if has_state:
            c_ref[...] = c0_ref[...]
            n_ref[...] = n0_ref[...]
            m_ref[...] = m0_ref[...]
        else:
            c_ref[...] = jnp.zeros_like(c_ref)
            n_ref[...] = jnp.zeros_like(n_ref)
            m_ref[...] = jnp.zeros_like(m_ref)

    mask, mask_t = _scan_masks(d, lc)
    fwd = d == 0
    gates = g_ref[0, 0]
    gates_t = gt_ref[0]
    lf = gates[:, nh:2 * nh]
    lf_t = gates_t[nh:2 * nh, :]
    cum = jnp.dot(mask.astype(F32), lf, precision=HIGHEST, preferred_element_type=F32)
    cum_t = jnp.dot(lf_t, mask_t.astype(F32), precision=HIGHEST, preferred_element_type=F32)
    tot = jnp.sum(lf, axis=0, keepdims=True)
    for h in range(nh):
        sl = slice(h * ML_DH, (h + 1) * ML_DH)
        bcol = cum[:, h:h + 1]
        brow = cum_t[h:h + 1, :]
        irow = gates_t[h:h + 1, :]
        m_prev = m_ref[0, 0, h:h + 1, 0:1]
        dm = jnp.where(mask, bcol - brow + irow, -jnp.inf)
        m_inter = bcol + m_prev
        m_i = jnp.maximum(jnp.max(dm, axis=-1, keepdims=True), m_inter)
        wmat = jnp.exp(dm - m_i)
        g_inter = jnp.exp(m_inter - m_i)
        qh = q_ref[0, :, sl]
        kth = kt_ref[0, h]
        vh = v_ref[0, :, sl]
        s = _dot(qh, kth) * wmat
        c_old = c_ref[0, 0, h]
        n_old = n_ref[0, 0, h:h + 1, :]
        num = _dot(s.astype(BF16), vh) + g_inter * _dot(qh, c_old.astype(BF16))
        qn = _dot_nt(qh, jnp.broadcast_to(n_old, (SUBLANES, ML_DH)).astype(BF16))[:, 0:1]
        den = jnp.sum(s, axis=-1, keepdims=True) + g_inter * qn
        denom = jnp.maximum(jnp.abs(den), jnp.exp(-m_i))
        h_ref[0, 0, :, sl] = num / denom
        m_new = jnp.where(fwd, m_i[lc - 1:lc, :], m_i[0:1, :])
        tot_h = tot[:, h:h + 1]
        w_row = jnp.exp(tot_h - brow + irow - m_new)
        decay = jnp.exp(tot_h + m_prev - m_new)
        kw = (kth.astype(F32) * w_row).astype(BF16)
        c_ref[0, 0, h] = decay * c_old + _dot(kw, vh)
        w8 = jnp.broadcast_to(w_row, (SUBLANES, lc)).astype(BF16)
        n_ref[0, 0, h:h + 1, :] = decay * n_old + _dot_nt(w8, kth)[0:1, :]
        m_ref[0, 0, h:h + 1, :] = jnp.broadcast_to(m_new, (1, LANES))


def ml_scan(q, kt, v, gates2, gates_t, state):
    bsz, seq, _ = q.shape
    lc = min(ML_CHUNK, seq)
    nc = seq // lc
    cidx = lambda d, c: c + d * (nc - 1 - 2 * c)
    tok = pl.BlockSpec((1, lc, ML_INNER), lambda b, d, c: (b, cidx(d, c), 0))
    in_specs = [tok,
                pl.BlockSpec((1, ML_HEADS, ML_DH, lc), lambda b, d, c: (b, 0, 0, cidx(d, c))),
                tok,
                pl.BlockSpec((1, 1, lc, 2 * ML_HEADS), lambda b, d, c: (d, b, cidx(d, c), 0)),
                pl.BlockSpec((1, 2 * ML_HEADS, lc), lambda b, d, c: (b, d, cidx(d, c)))]
    args = [q, kt, v, gates2, gates_t]
    st_specs = [pl.BlockSpec((1, 1, ML_HEADS, ML_DH, ML_DH), lambda b, d, c: (b, d, 0, 0, 0)),
                pl.BlockSpec((1, 1, ML_HEADS, ML_DH), lambda b, d, c: (b, d, 0, 0)),
                pl.BlockSpec((1, 1, SUBLANES, LANES), lambda b, d, c: (b, d, 0, 0))]
    if state is not None:
        in_specs += st_specs
        args += list(state)
    return pl.pallas_call(
        functools.partial(_ml_scan_body, has_state=state is not None),
        grid=(bsz, 2, nc),
        in_specs=in_specs,
        out_specs=[pl.BlockSpec((1, 1, lc, ML_INNER), lambda b, d, c: (d, b, cidx(d, c), 0))] + st_specs,
        out_shape=[jax.ShapeDtypeStruct((2, bsz, seq, ML_INNER), F32),
                   jax.ShapeDtypeStruct((bsz, 2, ML_HEADS, ML_DH, ML_DH), F32),
                   jax.ShapeDtypeStruct((bsz, 2, ML_HEADS, ML_DH), F32),
                   jax.ShapeDtypeStruct((bsz, 2, SUBLANES, LANES), F32)],
        compiler_params=_cparams("parallel", "arbitrary", "arbitrary"),
        name="ml_scan",
    )(*args)


def _ml_gate_act(r, transposed):
    idx = lax.broadcasted_iota(jnp.int32, r.shape, 0 if transposed else 1)
    return jnp.where((idx // ML_HEADS) % 2 == 1, _log_sigmoid(r), r)


_ML_PIECES = (_Piece(ML_INNER), _Piece(ML_INNER), _Piece(ML_INNER),
              _Piece(4 * ML_HEADS, bias=True, act=_ml_gate_act),
              _Piece(4 * ML_HEADS, transposed=True, bias=True, act=_ml_gate_act))


def _ml_weights(w_in, gate_b, conv_w, conv_b, w_q, w_k, w_v, norm_g, skip, w_out):
    w_g = w_in[:, 3 * ML_INNER:]
    operands = (w_in[:, :ML_INNER].astype(BF16), w_in[:, ML_INNER:2 * ML_INNER].astype(BF16),
                w_in[:, 2 * ML_INNER:3 * ML_INNER].astype(BF16),
                w_g.astype(BF16), gate_b.reshape(1, -1), w_g.T.astype(BF16), gate_b.reshape(-1, 1))
    return (operands, conv_w, conv_b, w_q.astype(BF16), jnp.swapaxes(w_k, 1, 2).astype(BF16), w_v.astype(BF16),
            norm_g.reshape(1, ML_INNER), skip.reshape(1, ML_INNER), w_out.astype(BF16))


def ml_layer(x, g, mod, per_batch, wts, state):
    operands, conv_w, conv_b, wq, wkt, wv, norm_g, skip, w_out = wts
    xm, o_pre, z, gates, gates_t = inproj(x, g, mod, _ML_PIECES, operands, per_batch)
    xc, q, kt, v = ml_qkv(xm, conv_w, conv_b, wq, wkt, wv)
    gates2 = jnp.stack([gates[..., :2 * ML_HEADS], gates[..., 2 * ML_HEADS:]], axis=0)
    if state is not None:
        c0, n0, m0 = state
        m0 = jnp.broadcast_to(jnp.pad(m0, ((0, 0), (0, 0), (0, SUBLANES - ML_HEADS)))[..., None],
                              m0.shape[:2] + (SUBLANES, LANES))
        state = (c0, n0, m0)
    hs, c_new, n_new, m_new = ml_scan(q, kt, v, gates2, gates_t, state)
    xn = outproj("ml", x, mod, per_batch, w_out, [o_pre, xc, z], dir_inputs=[hs], consts=[norm_g, skip])
    return xn, c_new, n_new, m_new[:, :, :ML_HEADS, 0]


def kernel(x_prompt, x_sample, cache_mla_ckv, cache_mla_kpe, state_ssd, state_mlstm_c, state_mlstm_n,
           state_mlstm_m, c, c_ctx, ada_w, ada_b, norm_g, final_norm_g,
           mla_w_in, mla_q_norm_g, mla_kv_norm_g, mla_w_uq, mla_w_uk, mla_w_uv, mla_w_out,
           ssd_w_in, ssd_conv_w, ssd_conv_b, ssd_dt_bias, ssd_a_log, ssd_d, ssd_norm_g, ssd_w_out,
           ml_w_in, ml_gate_b, ml_conv_w, ml_conv_b, ml_w_q, ml_w_k, ml_w_v, ml_norm_g, ml_skip, ml_w_out):
    nb = c.shape[0]
    rows = -(-(nb + 1) // SUBLANES) * SUBLANES
    cond = jnp.zeros((rows, D), F32).at[0].set(c_ctx).at[1:nb + 1].set(c)
    mods = adaln_all(cond, ada_w, ada_b)
    xp, xs = x_prompt, x_sample
    new_ckv, new_kpe, new_ssd, new_c, new_n, new_m = [], [], [], [], [], []
    yp = ys = None
    for i in range(DEPTH):
        kind, j = i % N_MIXERS, i // N_MIXERS
        mod_p = mods[i, 0:1].reshape(1, 1, 3 * D)
        mod_s = mods[i, 1:nb + 1].reshape(nb, 1, 3 * D)
        fg = final_norm_g if i == DEPTH - 1 else None
        if kind == 0:
            wts = _mla_weights(mla_w_in[j], mla_q_norm_g[j], mla_kv_norm_g[j], mla_w_uq[j], mla_w_uk[j], mla_w_uv[j])
            w_out = mla_w_out[j].astype(BF16)
            rp, ckv, kpe = mla_layer(xp, norm_g[i], mod_p, False, wts, w_out, False, final_g=fg)
            rs, _, _ = mla_layer(xs, norm_g[i], mod_s, True, wts, w_out, True,
                                 ctx=(cache_mla_ckv[:, j], cache_mla_kpe[:, j]), final_g=fg)
            new_ckv.append(ckv)
            new_kpe.append(kpe)
        elif kind == 1:
            wts = _ssd_weights(ssd_w_in[j], ssd_conv_w[j], ssd_conv_b[j], ssd_dt_bias[j], ssd_a_log[j], ssd_d[j],
                               ssd_norm_g[j], ssd_w_out[j])
            rp, st = ssd_layer(xp, norm_g[i], mod_p, False, wts, None)
            rs, _ = ssd_layer(xs, norm_g[i], mod_s, True, wts, state_ssd[:, j])
            new_ssd.append(st)
        else:
            wts = _ml_weights(ml_w_in[j], ml_gate_b[j], ml_conv_w[j], ml_conv_b[j], ml_w_q[j], ml_w_k[j], ml_w_v[j],
                              ml_norm_g[j], ml_skip[j], ml_w_out[j])
            rp, cc, nn, mm = ml_layer(xp, norm_g[i], mod_p, False, wts, None)
            rs, _, _, _ = ml_layer(xs, norm_g[i], mod_s, True, wts,
                                   (state_mlstm_c[:, j], state_mlstm_n[:, j], state_mlstm_m[:, j]))
            new_c.append(cc)
            new_n.append(nn)
            new_m.append(mm)
        if fg is not None:
            (xp, yp), (xs, ys) = rp, rs
        else:
            xp, xs = rp, rs
    return (yp, ys, jnp.stack(new_ckv, axis=1), jnp.stack(new_kpe, axis=1), jnp.stack(new_ssd, axis=1),
            jnp.stack(new_c, axis=1), jnp.stack(new_n, axis=1), jnp.stack(new_m, axis=1))
```

```python
import functools
import math

import jax
import jax.numpy as jnp
from jax import lax
from jax.experimental import pallas as pl
from jax.experimental.pallas import tpu as pltpu

F32 = jnp.float32
BF16 = jnp.bfloat16
HIGHEST = lax.Precision.HIGHEST

D = 1024
DEPTH = 4
GRID_W = 64
N_MIXERS = 3
NORM_EPS = 1e-6
CONV_W = 4

MLA_HEADS = 16
MLA_NOPE = 64
MLA_ROPE = 32
MLA_V = 64
MLA_Q_RANK = 384
MLA_KV_RANK = 256
MLA_SCALE = (MLA_NOPE + MLA_ROPE) ** -0.5
LOG2_E = math.log2(math.e)
ROPE_BASE = 10000.0

SSD_INNER = 2 * D
SSD_HEADDIM = 64
SSD_HEADS = SSD_INNER // SSD_HEADDIM
SSD_STATE = 128
SSD_GROUPS = 8
SSD_BC = SSD_GROUPS * SSD_STATE
SSD_CONV_DIM = SSD_INNER + 2 * SSD_BC
SSD_CHUNK = 128

ML_INNER = 2 * D
ML_HEADS = 4
ML_DH = ML_INNER // ML_HEADS

LANES = 128
SUBLANES = 8
VMEM_LIMIT = 56 * 1024 * 1024

TOKEN_TILE = 256
HALO = SUBLANES
ML_CHUNK = 256
ATTN_TQ = 512
ATTN_TK = 1536


def _cparams(*sem):
    return pltpu.CompilerParams(dimension_semantics=sem, vmem_limit_bytes=VMEM_LIMIT)


def _silu(v):
    return v * jax.nn.sigmoid(v)


def _softplus(v):
    return jnp.maximum(v, 0.0) + jnp.log(1.0 + jnp.exp(-jnp.abs(v)))


def _log_sigmoid(v):
    return jnp.minimum(v, 0.0) - jnp.log(1.0 + jnp.exp(-jnp.abs(v)))


def _modulate(x, g, mod):
    ms = jnp.mean(x * x, axis=-1, keepdims=True)
    y = x * lax.rsqrt(ms + NORM_EPS) * g
    return y * (1.0 + mod[:, D:2 * D]) + mod[:, 0:D]


def _dot(a, b):
    return jnp.dot(a, b, preferred_element_type=F32)


def _dot_nt(a, b):
    return lax.dot_general(a, b, (((1,), (1,)), ((), ())), preferred_element_type=F32)


def _const_spec(shape):
    nd = len(shape)
    return pl.BlockSpec(shape, lambda *_: (0,) * nd, pipeline_mode=pl.Buffered(1))


def _mod_spec(per_batch):
    if per_batch:
        return pl.BlockSpec((1, 1, 3 * D), lambda b, i: (b, 0, 0))
    return pl.BlockSpec((1, 1, 3 * D), lambda b, i: (0, 0, 0))


def _adaln_body(cond_ref, w_ref, b_ref, o_ref):
    a = _silu(cond_ref[...]).astype(BF16)
    o_ref[0] = _dot(a, w_ref[0].astype(BF16)) + b_ref[0]


def adaln_all(cond, ada_w, ada_b):
    r = cond.shape[0]
    return pl.pallas_call(
        _adaln_body,
        grid=(DEPTH, 3),
        in_specs=[pl.BlockSpec((r, D), lambda i, j: (0, 0)),
                  pl.BlockSpec((1, D, D), lambda i, j: (i, 0, j)),
                  pl.BlockSpec((1, 1, D), lambda i, j: (i, 0, j))],
        out_specs=pl.BlockSpec((1, r, D), lambda i, j: (i, 0, j)),
        out_shape=jax.ShapeDtypeStruct((DEPTH, r, 3 * D), F32),
        compiler_params=_cparams("arbitrary", "arbitrary"),
        name="adaln",
    )(cond, ada_w, ada_b.reshape(DEPTH, 1, 3 * D))


def _halo_specs(tm, width, seq):
    r = tm // HALO
    last = seq // HALO - 1
    return [pl.BlockSpec((1, tm, width), lambda b, i: (b, i, 0)),
            pl.BlockSpec((1, HALO, width), lambda b, i: (b, jnp.maximum(i * r - 1, 0), 0)),
            pl.BlockSpec((1, HALO, width), lambda b, i: (b, jnp.minimum((i + 1) * r, last), 0))]


def _conv_silu(main, prev8, next8, w, b, first, last):
    tm = main.shape[0]
    row = lax.broadcasted_iota(jnp.int32, main.shape, 0)
    before = jnp.where(first, 0.0, prev8[HALO - 1:HALO, :])
    after0 = jnp.where(last, 0.0, next8[0:1, :])
    after1 = jnp.where(last, 0.0, next8[1:2, :])
    xm1 = jnp.where(row == 0, before, pltpu.roll(main, 1, 0))
    xp1 = jnp.where(row == tm - 1, after0, pltpu.roll(main, tm - 1, 0))
    xp2 = jnp.where(row == tm - 2, after0, jnp.where(row == tm - 1, after1, pltpu.roll(main, tm - 2, 0)))
    y = w[0:1, :] * xm1 + w[1:2, :] * main + w[2:3, :] * xp1 + w[3:4, :] * xp2 + b
    return _silu(y)


def _rms(v, g):
    return v * lax.rsqrt(jnp.mean(v * v, axis=-1, keepdims=True) + NORM_EPS) * g


def _outproj_body(*refs, kind, final_norm):
    if kind == "mla":
        o_ref, z_ref, x_ref, mod_ref, w_ref = refs[:5]
        rest = refs[5:]
        pre = o_ref[0].astype(F32) * _silu(z_ref[0].astype(F32))
    elif kind == "ssd":
        yf_ref, yb_ref, xs_ref, z_ref, x_ref, mod_ref, w_ref, dsk_ref, ng_ref = refs[:9]
        rest = refs[9:]
        y = yf_ref[0, 0].astype(F32) + yb_ref[0, 0].astype(F32) + dsk_ref[...] * xs_ref[0]
        pre = _rms(y * _silu(z_ref[0].astype(F32)), ng_ref[...])
    else:
        hf_ref, hb_ref, op_ref, xc_ref, z_ref, x_ref, mod_ref, w_ref, ng_ref, sk_ref = refs[:10]
        rest = refs[10:]
        hh = (hf_ref[0, 0].astype(F32) + hb_ref[0, 0].astype(F32)) * jax.nn.sigmoid(op_ref[0].astype(F32))
        parts = []
        for h in range(ML_HEADS):
            v = hh[:, h * ML_DH:(h + 1) * ML_DH]
            mu = jnp.mean(v, axis=-1, keepdims=True)
            vc = v - mu
            var = jnp.mean(vc * vc, axis=-1, keepdims=True)
            parts.append(vc * lax.rsqrt(var + NORM_EPS))
        hn = jnp.concatenate(parts, axis=-1) * ng_ref[...]
        pre = (hn + sk_ref[...] * xc_ref[0]) * _silu(z_ref[0].astype(F32))
    out = _dot(pre.astype(BF16), w_ref[...])
    xn = x_ref[0] + mod_ref[0][:, 2 * D:3 * D] * out
    if final_norm:
        fg_ref, xo_ref, yo_ref = rest
        xo_ref[0] = xn
        yo_ref[0] = _rms(xn, fg_ref[...])
    else:
        (xo_ref,) = rest
        xo_ref[0] = xn


def outproj(kind, x, mod, per_batch, w_out, tok_inputs, dir_inputs=(), consts=(), final_g=None):
    bsz, seq, _ = x.shape
    tm = TOKEN_TILE
    args, in_specs = [], []
    for a in dir_inputs:
        wd = a.shape[-1]
        for d in range(2):
            args.append(a)
            in_specs.append(pl.BlockSpec((1, 1, tm, wd), functools.partial(lambda b, i, d: (d, b, i, 0), d=d)))
    for a in tok_inputs:
        args.append(a)
        in_specs.append(pl.BlockSpec((1, tm, a.shape[-1]), lambda b, i: (b, i, 0)))
    args += [x, mod, w_out]
    in_specs += [pl.BlockSpec((1, tm, D), lambda b, i: (b, i, 0)), _mod_spec(per_batch), _const_spec(w_out.shape)]
    for a in consts:
        args.append(a)
        in_specs.append(_const_spec(a.shape))
    tok_spec = pl.BlockSpec((1, tm, D), lambda b, i: (b, i, 0))
    shp = jax.ShapeDtypeStruct((bsz, seq, D), F32)
    if final_g is not None:
        args.append(final_g.reshape(1, D))
        in_specs.append(_const_spec((1, D)))
        out_specs, out_shape = [tok_spec, tok_spec], [shp, shp]
    else:
        out_specs, out_shape = [tok_spec], [shp]
    res = pl.pallas_call(
        functools.partial(_outproj_body, kind=kind, final_norm=final_g is not None),
        grid=(bsz, seq // tm),
        in_specs=in_specs, out_specs=out_specs, out_shape=out_shape,
        compiler_params=_cparams("parallel", "parallel"),
        name="outproj_" + kind,
    )(*args)
    return res if final_g is not None else res[0]


def _mla_pre_body(x_ref, g_ref, mod_ref, wcq_ref, wckv_ref, wkpe_ref, wkpes_ref, wz_ref, qg_ref, kvg_ref,
                  wuq1_ref, wuq2_ref, cq_ref, sq_ref, ck_ref, sk_ref,
                  q_ref, ckv_ref, kraw_ref, krope_ref, z_ref):
    h = _modulate(x_ref[0], g_ref[...], mod_ref[0]).astype(BF16)
    z_ref[0] = _dot(h, wz_ref[...]).astype(BF16)
    ckv_ref[0] = _rms(_dot(h, wckv_ref[...]), kvg_ref[...])
    kraw = _dot(h, wkpe_ref[...])
    kraw_ref[0] = kraw
    krope_ref[0] = kraw * ck_ref[...] + _dot(h, wkpes_ref[...]) * sk_ref[...]
    cq = _rms(_dot(h, wcq_ref[...]), qg_ref[...]).astype(BF16)
    cosq = cq_ref[...]
    sinq = sq_ref[...]
    for c0 in range(0, MLA_HEADS * LANES, 512):
        a = _dot(cq, wuq1_ref[:, c0:c0 + 512])
        b = _dot(cq, wuq2_ref[:, c0:c0 + 512])
        for s in range(0, 512, LANES):
            q_ref[0, :, c0 + s:c0 + s + LANES] = (
                a[:, s:s + LANES] * cosq + b[:, s:s + LANES] * sinq).astype(BF16)


def _head_slabs(w, off=0):
    k, nh, dh = w.shape
    return jnp.pad(w, ((0, 0), (0, 0), (off, LANES - off - dh))).reshape(k, nh * LANES)


def _swap_halves(w):
    half = w.shape[-1] // 2
    return jnp.concatenate([w[..., half:], w[..., :half]], axis=-1)


def _rope_tables(seq, use_rope):
    zeros32 = jnp.zeros((seq, LANES - MLA_NOPE - MLA_ROPE), F32)
    if use_rope:
        rows = seq // GRID_W
        row = jnp.repeat(jnp.arange(rows), GRID_W).astype(F32)
        col = jnp.tile(jnp.arange(GRID_W), rows).astype(F32)
        half = MLA_ROPE // 2
        inv = 1.0 / (ROPE_BASE ** (jnp.arange(0, half, 2, dtype=F32) / half))
        ang = jnp.concatenate([row[:, None] * inv, col[:, None] * inv], axis=-1)
        cos, sin = jnp.cos(ang), jnp.sin(ang)
    else:
        cos = jnp.ones((seq, MLA_ROPE // 2), F32)
        sin = jnp.zeros((seq, MLA_ROPE // 2), F32)
    ones64 = jnp.ones((seq, MLA_NOPE), F32)
    zeros64 = jnp.zeros((seq, MLA_NOPE), F32)
    cos_k = jnp.concatenate([zeros64, cos, cos, zeros32], axis=-1)
    sin_k = jnp.concatenate([zeros64, -sin, sin, zeros32], axis=-1)
    q_scale = MLA_SCALE * LOG2_E
    cos_q = jnp.concatenate([ones64, cos, cos, zeros32], axis=-1) * q_scale
    sin_q = sin_k * q_scale
    return cos_q, sin_q, cos_k, sin_k


def mla_pre(x, g, mod, per_batch, wts, tables):
    bsz, seq, _ = x.shape
    tm = TOKEN_TILE
    tok = lambda w, dt: (pl.BlockSpec((1, tm, w), lambda b, i: (b, i, 0)), jax.ShapeDtypeStruct((bsz, seq, w), dt))
    outs = [tok(MLA_HEADS * LANES, BF16), tok(MLA_KV_RANK, F32), tok(LANES, F32), tok(LANES, F32), tok(D, BF16)]
    tab_spec = pl.BlockSpec((tm, LANES), lambda b, i: (i, 0))
    in_specs = ([pl.BlockSpec((1, tm, D), lambda b, i: (b, i, 0)), _const_spec((1, D)), _mod_spec(per_batch)]
                + [_const_spec(w.shape) for w in wts] + [tab_spec] * 4)
    return pl.pallas_call(
        _mla_pre_body,
        grid=(bsz, seq // tm),
        in_specs=in_specs,
        out_specs=[o[0] for o in outs], out_shape=[o[1] for o in outs],
        compiler_params=_cparams("parallel", "parallel"),
        name="mla_pre",
    )(x, g.reshape(1, D), mod, *wts, *tables)


def _kv_expand_body(ckv_ref, kpe_ref, wkt_ref, wv_ref, kt_ref, v_ref):
    c = ckv_ref[0].astype(BF16)
    v_ref[0] = _dot(c, wv_ref[...]).astype(BF16)
    kpe_t = kpe_ref[0].T
    for c0 in range(0, MLA_HEADS * LANES, 512):
        a = _dot_nt(wkt_ref[c0:c0 + 512, :], c)
        for s in range(0, 512, LANES):
            kt_ref[0, 0, c0 + s:c0 + s + LANES, :] = (a[s:s + LANES, :] + kpe_t).astype(BF16)


def kv_expand(ckv, kpe_slab, wkt_aug, wv, tk):
    bsz, seq, _ = ckv.shape
    return pl.pallas_call(
        _kv_expand_body,
        grid=(bsz, seq // tk),
        in_specs=[pl.BlockSpec((1, tk, MLA_KV_RANK), lambda b, i: (b, i, 0)),
                  pl.BlockSpec((1, tk, LANES), lambda b, i: (b, i, 0)),
                  _const_spec(wkt_aug.shape), _const_spec(wv.shape)],
        out_specs=[pl.BlockSpec((1, 1, MLA_HEADS * LANES, tk), lambda b, i: (b, i, 0, 0)),
                   pl.BlockSpec((1, tk, MLA_HEADS * MLA_V), lambda b, i: (b, i, 0))],
        out_shape=[jax.ShapeDtypeStruct((bsz, seq // tk, MLA_HEADS * LANES, tk), BF16),
                   jax.ShapeDtypeStruct((bsz, seq, MLA_HEADS * MLA_V), BF16)],
        compiler_params=_cparams("parallel", "parallel"),
        name="mla_kv_expand",
    )(ckv, kpe_slab, wkt_aug, wv)


ATTN_ROWS = 16


def _attn_body(q_ref, kt_ref, v_ref, o_ref, s_ref, p_ref, m_ref, l_ref, a_ref, acc_ref, *, nk):
    tq = q_ref.shape[1]
    tk = kt_ref.shape[3]
    rb = ATTN_ROWS
    ncol = tk // LANES
    left = lax.broadcasted_iota(jnp.int32, (tq, LANES), 1) < MLA_V

    def scores(kc):
        for hh in range(2):
            s_ref[kc % 2, hh] = _dot(q_ref[0, :, hh * LANES:(hh + 1) * LANES],
                                     kt_ref[0, kc, hh * LANES:(hh + 1) * LANES, :])

    def softmax(kc):
        sl = kc % 2
        for hh in range(2):
            for r in range(0, tq, rb):
                mx = s_ref[sl, hh, r:r + rb, 0:LANES]
                for cc in range(1, ncol):
                    mx = jnp.maximum(mx, s_ref[sl, hh, r:r + rb, cc * LANES:(cc + 1) * LANES])
                mx = jnp.broadcast_to(jnp.max(mx, axis=-1, keepdims=True), (rb, LANES))
                if kc > 0:
                    m_old = m_ref[hh, r:r + rb, :]
                    mx = jnp.maximum(m_old, mx)
                    a_ref[hh, r:r + rb, :] = jnp.exp2(m_old - mx)
                m_ref[hh, r:r + rb, :] = mx
            for r in range(0, tq, rb):
                m_new = m_ref[hh, r:r + rb, :]
                tot = None
                for cc in range(ncol):
                    p = jnp.exp2(s_ref[sl, hh, r:r + rb, cc * LANES:(cc + 1) * LANES] - m_new)
                    p_ref[hh, r:r + rb, cc * LANES:(cc + 1) * LANES] = p.astype(BF16)
                    tot = p if tot is None else tot + p
                tot = jnp.broadcast_to(jnp.sum(tot, axis=-1, keepdims=True), (rb, LANES))
                if kc > 0:
                    tot = a_ref[hh, r:r + rb, :] * l_ref[hh, r:r + rb, :] + tot
                l_ref[hh, r:r + rb, :] = tot

    def values(kc):
        vv = v_ref[0, kc * tk:(kc + 1) * tk, :]
        pv = jnp.where(left, _dot(p_ref[0], vv), _dot(p_ref[1], vv))
        if kc > 0:
            pv = acc_ref[...] * jnp.where(left, a_ref[0], a_ref[1]) + pv
        acc_ref[...] = pv

    scores(0)
    for kc in range(nk):
        if kc + 1 < nk:
            scores(kc + 1)
        softmax(kc)
        values(kc)
    o_ref[0] = (acc_ref[...] / jnp.where(left, l_ref[0], l_ref[1])).astype(BF16)


def mla_attention(q_aug, kt, v):
    bsz, seq, _ = q_aug.shape
    _, nk, _, tk = kt.shape
    lk = nk * tk
    tq = min(ATTN_TQ, seq)
    stat = pltpu.VMEM((2, tq, LANES), F32)
    return pl.pallas_call(
        functools.partial(_attn_body, nk=nk),
        grid=(bsz, MLA_HEADS // 2, seq // tq),
        in_specs=[pl.BlockSpec((1, tq, 2 * LANES), lambda b, h, i: (b, i, h)),
                  pl.BlockSpec((1, nk, 2 * LANES, tk), lambda b, h, i: (b, 0, h, 0)),
                  pl.BlockSpec((1, lk, LANES), lambda b, h, i: (b, 0, h))],
        out_specs=pl.BlockSpec((1, tq, LANES), lambda b, h, i: (b, i, h)),
        out_shape=jax.ShapeDtypeStruct((bsz, seq, MLA_HEADS * MLA_V), BF16),
        scratch_shapes=[pltpu.VMEM((2, 2, tq, tk), F32), pltpu.VMEM((2, tq, tk), BF16), stat, stat, stat,
                        pltpu.VMEM((tq, LANES), F32)],
        compiler_params=_cparams("parallel", "parallel", "arbitrary"),
        name="mla_attention",
    )(q_aug, kt, v)


def _mla_weights(w_in, q_norm_g, kv_norm_g, w_uq, w_uk, w_uv):
    o1, o2, o3 = MLA_Q_RANK, MLA_Q_RANK + MLA_KV_RANK, MLA_Q_RANK + MLA_KV_RANK + MLA_ROPE
    w_kpe = w_in[:, o2:o3]
    pad_k = lambda w: jnp.pad(w, ((0, 0), (MLA_NOPE, LANES - MLA_NOPE - MLA_ROPE)))
    uq_pe = w_uq[..., MLA_NOPE:]
    wuq1 = _head_slabs(w_uq)
    wuq2 = _head_slabs(_swap_halves(uq_pe), off=MLA_NOPE)
    pre = (w_in[:, :o1].astype(BF16), w_in[:, o1:o2].astype(BF16), pad_k(w_kpe).astype(BF16),
           pad_k(_swap_halves(w_kpe)).astype(BF16), w_in[:, o3:].astype(BF16),
           q_norm_g.reshape(1, -1), kv_norm_g.reshape(1, -1), wuq1.astype(BF16), wuq2.astype(BF16))
    wkt_aug = _head_slabs(w_uk).T.astype(BF16)
    wv = w_uv.reshape(MLA_KV_RANK, MLA_HEADS * MLA_V).astype(BF16)
    return pre, wkt_aug, wv


def _key_chunk(lk):
    tk = min(ATTN_TK, lk)
    while lk % tk:
        tk -= LANES
    return tk


def mla_layer(x, g, mod, per_batch, wts, w_out, use_rope, ctx=None, final_g=None):
    pre_w, wkt_aug, wv = wts
    seq = x.shape[1]
    q_aug, ckv, kraw, krope, z = mla_pre(x, g, mod, per_batch, pre_w, _rope_tables(seq, use_rope))
    ckv_all, kpe_all = ckv, krope
    if ctx is not None:
        ctx_ckv, ctx_kpe = ctx
        ckv_all = jnp.concatenate([ckv, ctx_ckv], axis=1)
        kpe_all = jnp.concatenate(
            [krope, jnp.pad(ctx_kpe, ((0, 0), (0, 0), (MLA_NOPE, LANES - MLA_NOPE - MLA_ROPE)))], axis=1)
    kt, v = kv_expand(ckv_all, kpe_all, wkt_aug, wv, _key_chunk(ckv_all.shape[1]))
    o = mla_attention(q_aug, kt, v)
    res = outproj("mla", x, mod, per_batch, w_out, [o, z], final_g=final_g)
    return res, ckv, kraw[..., MLA_NOPE:MLA_NOPE + MLA_ROPE]


DT_REP = 4 * SSD_HEADS


def _ssd_in_body(x_ref, xp_ref, xn_ref, g_ref, mod_ref, wz_ref, wx_ref, wdt_ref, bdt_ref, wdtt_ref, bdtt_ref,
                 cw_ref, cb_ref, z_ref, xs_ref, bt_ref, c_ref, dt_ref, dtt_ref):
    i = pl.program_id(1)
    first = i == 0
    last = i == pl.num_programs(1) - 1
    tm = x_ref.shape[1]
    g = g_ref[...]
    mod = mod_ref[0]
    h = _modulate(x_ref[0], g, mod)
    h_ext = jnp.concatenate([_modulate(xp_ref[0], g, mod), h, _modulate(xn_ref[0], g, mod)], axis=0).astype(BF16)
    hb = h.astype(BF16)
    for c0 in range(0, SSD_INNER, 1024):
        z_ref[0, :, c0:c0 + 1024] = _dot(hb, wz_ref[:, c0:c0 + 1024]).astype(BF16)
    dt_ref[0] = _softplus(_dot(hb, wdt_ref[...]) + bdt_ref[...])
    dtt_ref[0] = _softplus(_dot_nt(wdtt_ref[...], hb) + bdtt_ref[...])
    cw = 512
    for c0 in range(0, SSD_CONV_DIM, cw):
        e = _dot(h_ext, wx_ref[:, c0:c0 + cw])
        y = _conv_silu(e[HALO:tm + HALO, :], e[0:HALO, :], e[tm + HALO:tm + 2 * HALO, :],
                       cw_ref[:, c0:c0 + cw], cb_ref[:, c0:c0 + cw], first, last)
        if c0 < SSD_INNER:
            xs_ref[0, :, c0:c0 + cw] = y
        elif c0 < SSD_INNER + SSD_BC:
            o = c0 - SSD_INNER
            bt_ref[0, o:o + cw, :] = y.T.astype(BF16)
        else:
            o = c0 - SSD_INNER - SSD_BC
            c_ref[0, :, o:o + cw] = y.astype(BF16)


def ssd_in(x, g, mod, per_batch, operands):
    bsz, seq, _ = x.shape
    tm = TOKEN_TILE
    tok = lambda w: pl.BlockSpec((1, tm, w), lambda b, i: (b, i, 0))
    return pl.pallas_call(
        _ssd_in_body,
        grid=(bsz, seq // tm),
        in_specs=_halo_specs(tm, D, seq) + [_const_spec((1, D)), _mod_spec(per_batch)]
        + [_const_spec(op.shape) for op in operands],
        out_specs=[tok(SSD_INNER), tok(SSD_INNER), pl.BlockSpec((1, SSD_BC, tm), lambda b, i: (b, 0, i)),
                   tok(SSD_BC), tok(2 * DT_REP), pl.BlockSpec((1, 2 * SSD_HEADS, tm), lambda b, i: (b, 0, i))],
        out_shape=[jax.ShapeDtypeStruct((bsz, seq, SSD_INNER), BF16),
                   jax.ShapeDtypeStruct((bsz, seq, SSD_INNER), F32),
                   jax.ShapeDtypeStruct((bsz, SSD_BC, seq), BF16),
                   jax.ShapeDtypeStruct((bsz, seq, SSD_BC), BF16),
                   jax.ShapeDtypeStruct((bsz, seq, 2 * DT_REP), F32),
                   jax.ShapeDtypeStruct((bsz, 2 * SSD_HEADS, seq), F32)],
        compiler_params=_cparams("parallel", "parallel"),
        name="ssd_in",
    )(x, x, x, g.reshape(1, D), mod, *operands)


def _scan_masks(d, n):
    ri = lax.broadcasted_iota(jnp.int32, (n, n), 0)
    ci = lax.broadcasted_iota(jnp.int32, (n, n), 1)
    diff = (ri - ci) * (1 - 2 * d)
    return diff >= 0, diff <= 0


def _split3(v):
    lane = lax.broadcasted_iota(jnp.int32, v.shape, 1)
    hi = v.astype(BF16).astype(F32)
    r1 = v - hi
    mid = r1.astype(BF16).astype(F32)
    lo = r1 - mid
    w = SSD_HEADS
    return jnp.where(lane < w, hi, jnp.where(lane < 2 * w, mid, jnp.where(lane < 3 * w, lo, 0.0))).astype(BF16)


def _expand_matrix(width):
    k = jnp.arange(DT_REP)[:, None]
    col_head = jnp.arange(SSD_HEADS * width)[None, :] // width
    return ((k < 3 * SSD_HEADS) & (k % SSD_HEADS == col_head)).astype(BF16)


def _ssd_scan_body(*refs, has_h0):
    if has_h0:
        xs_ref, bt_ref, c_ref, dt_ref, dtt_ref, a_ref, at_ref, eq_ref, ep_ref, h0_ref, y_ref, hout_ref, ht_s = refs
    else:
        xs_ref, bt_ref, c_ref, dt_ref, dtt_ref, a_ref, at_ref, eq_ref, ep_ref, y_ref, hout_ref, ht_s = refs
    d = pl.program_id(1)
    c = pl.program_id(2)
    q = SSD_CHUNK
    hp_n = SSD_HEADS * SSD_HEADDIM

    @pl.when(c == 0)
    def _():
        if has_h0:
            ht_s[...] = h0_ref[0, 0].reshape(hp_n, SSD_STATE).T
        else:
            ht_s[...] = jnp.zeros_like(ht_s)

    mask, mask_t = _scan_masks(d, q)
    dt = dt_ref[0]
    dtt = dtt_ref[0]
    da = dt * a_ref[0]
    dat = dtt * at_ref[0]
    cum = jnp.dot(mask.astype(F32), da, precision=HIGHEST, preferred_element_type=F32)
    cum_t = jnp.dot(dat, mask_t.astype(F32), precision=HIGHEST, preferred_element_type=F32)
    tot = jnp.sum(da, axis=0, keepdims=True)
    cum3 = _split3(cum)
    ecum3 = _split3(jnp.exp(cum))
    wend3 = _split3(jnp.exp(tot - cum) * dt)
    left = lax.broadcasted_iota(jnp.int32, (q, LANES), 1) < SSD_HEADDIM
    heads_g = SSD_HEADS // SSD_GROUPS
    gw = heads_g * SSD_HEADDIM

    for g in range(SSD_GROUPS):
        cum_b = _dot(cum3, eq_ref[:, g * heads_g * q:(g + 1) * heads_g * q])
        ecum_b = _dot(ecum3, ep_ref[:, g * gw:(g + 1) * gw])
        wend_b = _dot(wend3, ep_ref[:, g * gw:(g + 1) * gw])
        decay_b = jnp.where(d == 0, ecum_b[q - 1:q, :], ecum_b[0:1, :])
        cg = c_ref[0, :, g * SSD_STATE:(g + 1) * SSD_STATE]
        btg = bt_ref[0, g * SSD_STATE:(g + 1) * SSD_STATE, :]
        cb = _dot(cg, btg)
        y_state = _dot(cg, ht_s[:, g * gw:(g + 1) * gw].astype(BF16))
        for pr in range(gw // LANES):
            lo = g * gw + pr * LANES
            sl = slice(pr * LANES, (pr + 1) * LANES)
            xs = xs_ref[0, :, lo:lo + LANES]
            xs_b = xs.astype(BF16)
            ys = []
            for hh in range(2):
                hl = 2 * pr + hh
                h = g * heads_g + hl
                seg = cum_b[:, hl * q:(hl + 1) * q] - cum_t[h:h + 1, :]
                lm = jnp.exp(jnp.where(mask, seg, -jnp.inf))
                mm = (cb * lm * dtt[h:h + 1, :]).astype(BF16)
                ys.append(_dot(mm, xs_b))
            y_ref[0, 0, :, lo:lo + LANES] = (jnp.where(left, ys[0], ys[1])
                                             + y_state[:, sl] * ecum_b[:, sl]).astype(BF16)
            xw = (xs * wend_b[:, sl]).astype(BF16)
            ht_s[:, lo:lo + LANES] = ht_s[:, lo:lo + LANES] * decay_b[:, sl] + _dot(btg, xw)

    @pl.when(c == pl.num_programs(2) - 1)
    def _():
        hout_ref[0, 0] = ht_s[...].T.reshape(SSD_HEADS, SSD_HEADDIM, SSD_STATE)


def ssd_scan(xs, bt, cm, dt_rep, dtt, a, h0):
    bsz, seq, _ = xs.shape
    q = SSD_CHUNK
    nc = seq // q
    cidx = lambda d, c: c + d * (nc - 1 - 2 * c)
    a_rep = jnp.concatenate([a, a, a, jnp.zeros_like(a)], axis=-1).reshape(2, 1, DT_REP)
    eq = _expand_matrix(q)
    ep = _expand_matrix(SSD_HEADDIM)
    in_specs = [pl.BlockSpec((1, q, SSD_INNER), lambda b, d, c: (b, cidx(d, c), 0)),
                pl.BlockSpec((1, SSD_BC, q), lambda b, d, c: (b, 0, cidx(d, c))),
                pl.BlockSpec((1, q, SSD_BC), lambda b, d, c: (b, cidx(d, c), 0)),
                pl.BlockSpec((1, q, DT_REP), lambda b, d, c: (b, cidx(d, c), d)),
                pl.BlockSpec((1, SSD_HEADS, q), lambda b, d, c: (b, d, cidx(d, c))),
                pl.BlockSpec((1, 1, DT_REP), lambda b, d, c: (d, 0, 0)),
                pl.BlockSpec((1, SSD_HEADS, 1), lambda b, d, c: (d, 0, 0)),
                _const_spec(eq.shape), _const_spec(ep.shape)]
    args = [xs, bt, cm, dt_rep, dtt, a_rep, a.reshape(2, SSD_HEADS, 1), eq, ep]
    st_spec = pl.BlockSpec((1, 1, SSD_HEADS, SSD_HEADDIM, SSD_STATE), lambda b, d, c: (b, d, 0, 0, 0))
    if h0 is not None:
        in_specs.append(st_spec)
        args.append(h0)
    return pl.pallas_call(
        functools.partial(_ssd_scan_body, has_h0=h0 is not None),
        grid=(bsz, 2, nc),
        in_specs=in_specs,
        out_specs=[pl.BlockSpec((1, 1, q, SSD_INNER), lambda b, d, c: (d, b, cidx(d, c), 0)), st_spec],
        out_shape=[jax.ShapeDtypeStruct((2, bsz, seq, SSD_INNER), BF16),
                   jax.ShapeDtypeStruct((bsz, 2, SSD_HEADS, SSD_HEADDIM, SSD_STATE), F32)],
        scratch_shapes=[pltpu.VMEM((SSD_STATE, SSD_INNER), F32)],
        compiler_params=_cparams("parallel", "arbitrary", "arbitrary"),
        name="ssd_scan",
    )(*args)


def _ssd_weights(w_in, conv_w, conv_b, dt_bias, a_log, d_skip, norm_g, w_out):
    o1, o2 = SSD_INNER, SSD_INNER + SSD_CONV_DIM
    w_dt = w_in[:, o2:]
    nh = SSD_HEADS
    rep = lambda t: jnp.concatenate([t[..., :nh]] * 3 + [jnp.zeros_like(t[..., :nh])]
                                    + [t[..., nh:]] * 3 + [jnp.zeros_like(t[..., :nh])], axis=-1)
    operands = (w_in[:, :o1].astype(BF16), w_in[:, o1:o2].astype(BF16),
                rep(w_dt).astype(BF16), rep(dt_bias.reshape(1, -1)),
                w_dt.T.astype(BF16), dt_bias.reshape(-1, 1),
                conv_w, conv_b.reshape(1, SSD_CONV_DIM))
    a = -jnp.exp(a_log.astype(F32))
    dsk = jnp.repeat(d_skip, SSD_HEADDIM).reshape(1, SSD_INNER)
    return operands, a, dsk, norm_g.reshape(1, SSD_INNER), w_out.astype(BF16)


def ssd_layer(x, g, mod, per_batch, wts, h0):
    operands, a, dsk, norm_g, w_out = wts
    z, xs, bt, cm, dt_rep, dtt = ssd_in(x, g, mod, per_batch, operands)
    y, h_new = ssd_scan(xs, bt, cm, dt_rep, dtt, a, h0)
    xn = outproj("ssd", x, mod, per_batch, w_out, [xs, z], dir_inputs=[y], consts=[dsk, norm_g])
    return xn, h_new


def _ml_gate_act(r, transposed):
    idx = lax.broadcasted_iota(jnp.int32, r.shape, 0 if transposed else 1)
    return jnp.where((idx // ML_HEADS) % 2 == 1, _log_sigmoid(r), r)


def _ml_in_body(x_ref, xp_ref, xn_ref, g_ref, mod_ref, wxm_ref, wo_ref, wz_ref, wg_ref, bg_ref, wgt_ref, bgt_ref,
                cw_ref, cb_ref, wq_ref, wkt_ref, wv_ref,
                xc_ref, q_ref, kt_ref, v_ref, o_ref, z_ref, gate_ref, gatet_ref):
    i = pl.program_id(1)
    first = i == 0
    last = i == pl.num_programs(1) - 1
    tm = x_ref.shape[1]
    g = g_ref[...]
    mod = mod_ref[0]
    h = _modulate(x_ref[0], g, mod)
    h_ext = jnp.concatenate([_modulate(xp_ref[0], g, mod), h, _modulate(xn_ref[0], g, mod)], axis=0).astype(BF16)
    hb = h.astype(BF16)
    for c0 in range(0, ML_INNER, 1024):
        o_ref[0, :, c0:c0 + 1024] = _dot(hb, wo_ref[:, c0:c0 + 1024]).astype(BF16)
        z_ref[0, :, c0:c0 + 1024] = _dot(hb, wz_ref[:, c0:c0 + 1024]).astype(BF16)
    gate_ref[0] = _ml_gate_act(_dot(hb, wg_ref[...]) + bg_ref[...], False)
    gatet_ref[0] = _ml_gate_act(_dot_nt(wgt_ref[...], hb) + bgt_ref[...], True)
    for hd in range(ML_HEADS):
        sl = slice(hd * ML_DH, (hd + 1) * ML_DH)
        e = _dot(h_ext, wxm_ref[:, sl])
        xm = e[HALO:tm + HALO, :]
        xc = _conv_silu(xm, e[0:HALO, :], e[tm + HALO:tm + 2 * HALO, :], cw_ref[:, sl], cb_ref[:, sl], first, last)
        xc_ref[0, :, sl] = xc
        xc_b = xc.astype(BF16)
        q_ref[0, :, sl] = _dot(xc_b, wq_ref[hd]).astype(BF16)
        kt_ref[0, hd] = (_dot_nt(wkt_ref[hd], xc_b) * (ML_DH ** -0.5)).astype(BF16)
        v_ref[0, :, sl] = _dot(xm.astype(BF16), wv_ref[hd]).astype(BF16)


def ml_in(x, g, mod, per_batch, operands):
    bsz, seq, _ = x.shape
    tm = TOKEN_TILE
    tok = pl.BlockSpec((1, tm, ML_INNER), lambda b, i: (b, i, 0))
    big = lambda dt: jax.ShapeDtypeStruct((bsz, seq, ML_INNER), dt)
    ng = 4 * ML_HEADS
    return pl.pallas_call(
        _ml_in_body,
        grid=(bsz, seq // tm),
        in_specs=_halo_specs(tm, D, seq) + [_const_spec((1, D)), _mod_spec(per_batch)]
        + [_const_spec(op.shape) for op in operands],
        out_specs=[tok, tok, pl.BlockSpec((1, ML_HEADS, ML_DH, tm), lambda b, i: (b, 0, 0, i)), tok, tok, tok,
                   pl.BlockSpec((1, tm, ng), lambda b, i: (b, i, 0)), pl.BlockSpec((1, ng, tm), lambda b, i: (b, 0, i))],
        out_shape=[big(F32), big(BF16), jax.ShapeDtypeStruct((bsz, ML_HEADS, ML_DH, seq), BF16), big(BF16),
                   big(BF16), big(BF16), jax.ShapeDtypeStruct((bsz, seq, ng), F32),
                   jax.ShapeDtypeStruct((bsz, ng, seq), F32)],
        compiler_params=_cparams("parallel", "parallel"),
        name="ml_in",
    )(x, x, x, g.reshape(1, D), mod, *operands)


def _ml_scan_body(*refs, has_state):
    if has_state:
        q_ref, kt_ref, v_ref, g_ref, gt_ref, c0_ref, n0_ref, m0_ref, h_ref, c_ref, n_ref, m_ref = refs
    else:
        q_ref, kt_ref, v_ref, g_ref, gt_ref, h_ref, c_ref, n_ref, m_ref = refs
    d = pl.program_id(1)
    c = pl.program_id(2)
    lc = q_ref.shape[1]
    nh = ML_HEADS

    @pl.when(c == 0)
    def _():
        if has_state:
            c_ref[...] = c0_ref[...]
            n_ref[...] = n0_ref[...]
            m_ref[...] = m0_ref[...]
        else:
            c_ref[...] = jnp.zeros_like(c_ref)
            n_ref[...] = jnp.zeros_like(n_ref)
            m_ref[...] = jnp.zeros_like(m_ref)

    mask, mask_t = _scan_masks(d, lc)
    fwd = d == 0
    gates = g_ref[0, 0]
    gates_t = gt_ref[0]
    lf = gates[:, nh:2 * nh]
    lf_t = gates_t[nh:2 * nh, :]
    cum = jnp.dot(mask.astype(F32), lf, precision=HIGHEST, preferred_element_type=F32)
    cum_t = jnp.dot(lf_t, mask_t.astype(F32), precision=HIGHEST, preferred_element_type=F32)
    tot = jnp.sum(lf, axis=0, keepdims=True)
    for h in range(nh):
        sl = slice(h * ML_DH, (h + 1) * ML_DH)
        bcol = cum[:, h:h + 1]
        brow = cum_t[h:h + 1, :]
        irow = gates_t[h:h + 1, :]
        m_prev = m_ref[0, 0, h:h + 1, 0:1]
        dm = jnp.where(mask, bcol - brow + irow, -jnp.inf)
        m_inter = bcol + m_prev
        m_i = jnp.maximum(jnp.max(dm, axis=-1, keepdims=True), m_inter)
        wmat = jnp.exp(dm - m_i)
        g_inter = jnp.exp(m_inter - m_i)
        qh = q_ref[0, :, sl]
        kth = kt_ref[0, h]
        vh = v_ref[0, :, sl]
        s = _dot(qh, kth) * wmat
        c_old = c_ref[0, 0, h]
        n_old = n_ref[0, 0, h:h + 1, :]
        num = _dot(s.astype(BF16), vh) + g_inter * _dot(qh, c_old.astype(BF16))
        qn = _dot_nt(qh, jnp.broadcast_to(n_old, (SUBLANES, ML_DH)).astype(BF16))[:, 0:1]
        den = jnp.sum(s, axis=-1, keepdims=True) + g_inter * qn
        denom = jnp.maximum(jnp.abs(den), jnp.exp(-m_i))
        h_ref[0, 0, :, sl] = (num / denom).astype(BF16)
        m_new = jnp.where(fwd, m_i[lc - 1:lc, :], m_i[0:1, :])
        tot_h = tot[:, h:h + 1]
        w_row = jnp.exp(tot_h - brow + irow - m_new)
        decay = jnp.exp(tot_h + m_prev - m_new)
        kw = (kth.astype(F32) * w_row).astype(BF16)
        c_ref[0, 0, h] = decay * c_old + _dot(kw, vh)
        w8 = jnp.broadcast_to(w_row, (SUBLANES, lc)).astype(BF16)
        n_ref[0, 0, h:h + 1, :] = decay * n_old + _dot_nt(w8, kth)[0:1, :]
        m_ref[0, 0, h:h + 1, :] = jnp.broadcast_to(m_new, (1, LANES))


def ml_scan(q, kt, v, gates2, gates_t, state):
    bsz, seq, _ = q.shape
    lc = min(ML_CHUNK, seq)
    nc = seq // lc
    cidx = lambda d, c: c + d * (nc - 1 - 2 * c)
    tok = pl.BlockSpec((1, lc, ML_INNER), lambda b, d, c: (b, cidx(d, c), 0))
    in_specs = [tok,
                pl.BlockSpec((1, ML_HEADS, ML_DH, lc), lambda b, d, c: (b, 0, 0, cidx(d, c))),
                tok,
                pl.BlockSpec((1, 1, lc, 2 * ML_HEADS), lambda b, d, c: (d, b, cidx(d, c), 0)),
                pl.BlockSpec((1, 2 * ML_HEADS, lc), lambda b, d, c: (b, d, cidx(d, c)))]
    args = [q, kt, v, gates2, gates_t]
    st_specs = [pl.BlockSpec((1, 1, ML_HEADS, ML_DH, ML_DH), lambda b, d, c: (b, d, 0, 0, 0)),
                pl.BlockSpec((1, 1, ML_HEADS, ML_DH), lambda b, d, c: (b, d, 0, 0)),
                pl.BlockSpec((1, 1, SUBLANES, LANES), lambda b, d, c: (b, d, 0, 0))]
    if state is not None:
        in_specs += st_specs
        args += list(state)
    return pl.pallas_call(
        functools.partial(_ml_scan_body, has_state=state is not None),
        grid=(bsz, 2, nc),
        in_specs=in_specs,
        out_specs=[pl.BlockSpec((1, 1, lc, ML_INNER), lambda b, d, c: (d, b, cidx(d, c), 0))] + st_specs,
        out_shape=[jax.ShapeDtypeStruct((2, bsz, seq, ML_INNER), BF16),
                   jax.ShapeDtypeStruct((bsz, 2, ML_HEADS, ML_DH, ML_DH), F32),
                   jax.ShapeDtypeStruct((bsz, 2, ML_HEADS, ML_DH), F32),
                   jax.ShapeDtypeStruct((bsz, 2, SUBLANES, LANES), F32)],
        compiler_params=_cparams("parallel", "arbitrary", "arbitrary"),
        name="ml_scan",
    )(*args)


def _ml_weights(w_in, gate_b, conv_w, conv_b, w_q, w_k, w_v, norm_g, skip, w_out):
    w_g = w_in[:, 3 * ML_INNER:]
    operands = (w_in[:, :ML_INNER].astype(BF16), w_in[:, ML_INNER:2 * ML_INNER].astype(BF16),
                w_in[:, 2 * ML_INNER:3 * ML_INNER].astype(BF16),
                w_g.astype(BF16), gate_b.reshape(1, -1), w_g.T.astype(BF16), gate_b.reshape(-1, 1),
                conv_w, conv_b.reshape(1, ML_INNER),
                w_q.astype(BF16), jnp.swapaxes(w_k, 1, 2).astype(BF16), w_v.astype(BF16))
    return operands, norm_g.reshape(1, ML_INNER), skip.reshape(1, ML_INNER), w_out.astype(BF16)


def ml_layer(x, g, mod, per_batch, wts, state):
    operands, norm_g, skip, w_out = wts
    xc, q, kt, v, o_pre, z, gates, gates_t = ml_in(x, g, mod, per_batch, operands)
    gates2 = jnp.stack([gates[..., :2 * ML_HEADS], gates[..., 2 * ML_HEADS:]], axis=0)
    if state is not None:
        c0, n0, m0 = state
        m0 = jnp.broadcast_to(jnp.pad(m0, ((0, 0), (0, 0), (0, SUBLANES - ML_HEADS)))[..., None],
                              m0.shape[:2] + (SUBLANES, LANES))
        state = (c0, n0, m0)
    hs, c_new, n_new, m_new = ml_scan(q, kt, v, gates2, gates_t, state)
    xn = outproj("ml", x, mod, per_batch, w_out, [o_pre, xc, z], dir_inputs=[hs], consts=[norm_g, skip])
    return xn, c_new, n_new, m_new[:, :, :ML_HEADS, 0]


def kernel(x_prompt, x_sample, cache_mla_ckv, cache_mla_kpe, state_ssd, state_mlstm_c, state_mlstm_n,
           state_mlstm_m, c, c_ctx, ada_w, ada_b, norm_g, final_norm_g,
           mla_w_in, mla_q_norm_g, mla_kv_norm_g, mla_w_uq, mla_w_uk, mla_w_uv, mla_w_out,
           ssd_w_in, ssd_conv_w, ssd_conv_b, ssd_dt_bias, ssd_a_log, ssd_d, ssd_norm_g, ssd_w_out,
           ml_w_in, ml_gate_b, ml_conv_w, ml_conv_b, ml_w_q, ml_w_k, ml_w_v, ml_norm_g, ml_skip, ml_w_out):
    nb = c.shape[0]
    rows = -(-(nb + 1) // SUBLANES) * SUBLANES
    cond = jnp.zeros((rows, D), F32).at[0].set(c_ctx).at[1:nb + 1].set(c)
    mods = adaln_all(cond, ada_w, ada_b)
    xp, xs = x_prompt, x_sample
    new_ckv, new_kpe, new_ssd, new_c, new_n, new_m = [], [], [], [], [], []
    yp = ys = None
    for i in range(DEPTH):
        kind, j = i % N_MIXERS, i // N_MIXERS
        mod_p = mods[i, 0:1].reshape(1, 1, 3 * D)
        mod_s = mods[i, 1:nb + 1].reshape(nb, 1, 3 * D)
        fg = final_norm_g if i == DEPTH - 1 else None
        if kind == 0:
            wts = _mla_weights(mla_w_in[j], mla_q_norm_g[j], mla_kv_norm_g[j], mla_w_uq[j], mla_w_uk[j], mla_w_uv[j])
            w_out = mla_w_out[j].astype(BF16)
            rp, ckv, kpe = mla_layer(xp, norm_g[i], mod_p, False, wts, w_out, False, final_g=fg)
            rs, _, _ = mla_layer(xs, norm_g[i], mod_s, True, wts, w_out, True,
                                 ctx=(cache_mla_ckv[:, j], cache_mla_kpe[:, j]), final_g=fg)
            new_ckv.append(ckv)
            new_kpe.append(kpe)
        elif kind == 1:
            wts = _ssd_weights(ssd_w_in[j], ssd_conv_w[j], ssd_conv_b[j], ssd_dt_bias[j], ssd_a_log[j], ssd_d[j],
                               ssd_norm_g[j], ssd_w_out[j])
            rp, st = ssd_layer(xp, norm_g[i], mod_p, False, wts, None)
            rs, _ = ssd_layer(xs, norm_g[i], mod_s, True, wts, state_ssd[:, j])
            new_ssd.append(st)
        else:
            wts = _ml_weights(ml_w_in[j], ml_gate_b[j], ml_conv_w[j], ml_conv_b[j], ml_w_q[j], ml_w_k[j], ml_w_v[j],
                              ml_norm_g[j], ml_skip[j], ml_w_out[j])
            rp, cc, nn, mm = ml_layer(xp, norm_g[i], mod_p, False, wts, None)
            rs, _, _, _ = ml_layer(xs, norm_g[i], mod_s, True, wts,
                                   (state_mlstm_c[:, j], state_mlstm_n[:, j], state_mlstm_m[:, j]))
            new_c.append(cc)
            new_n.append(nn)
            new_m.append(mm)
        if fg is not None:
            (xp, yp), (xs, ys) = rp, rs
        else:
            xp, xs = rp, rs
    return (yp, ys, jnp.stack(new_ckv, axis=1), jnp.stack(new_kpe, axis=1), jnp.stack(new_ssd, axis=1),
            jnp.stack(new_c, axis=1), jnp.stack(new_n, axis=1), jnp.stack(new_m, axis=1))
```

```python
import functools
import math

import jax
import jax.numpy as jnp
from jax import lax
from jax.experimental import pallas as pl
from jax.experimental.pallas import tpu as pltpu

F32 = jnp.float32
BF16 = jnp.bfloat16
HIGHEST = lax.Precision.HIGHEST

D = 1024
DEPTH = 4
GRID_W = 64
N_MIXERS = 3
NORM_EPS = 1e-6
CONV_W = 4

MLA_HEADS = 16
MLA_NOPE = 64
MLA_ROPE = 32
MLA_V = 64
MLA_Q_RANK = 384
MLA_KV_RANK = 256
MLA_SCALE = (MLA_NOPE + MLA_ROPE) ** -0.5
LOG2_E = math.log2(math.e)
ROPE_BASE = 10000.0

SSD_INNER = 2 * D
SSD_HEADDIM = 64
SSD_HEADS = SSD_INNER // SSD_HEADDIM
SSD_STATE = 128
SSD_GROUPS = 8
SSD_BC = SSD_GROUPS * SSD_STATE
SSD_CONV_DIM = SSD_INNER + 2 * SSD_BC
SSD_CHUNK = 128

ML_INNER = 2 * D
ML_HEADS = 4
ML_DH = ML_INNER // ML_HEADS

LANES = 128
SUBLANES = 8
VMEM_LIMIT = 56 * 1024 * 1024

TOKEN_TILE = 256
HALO = SUBLANES
CONV_COLS = 512
ML_CHUNK = TOKEN_TILE
ATTN_TQ = 1024
ATTN_TK = 1536


def _cparams(*sem):
    return pltpu.CompilerParams(dimension_semantics=sem, vmem_limit_bytes=VMEM_LIMIT)


def _silu(v):
    return v * jax.nn.sigmoid(v)


def _softplus(v):
    return jnp.maximum(v, 0.0) + jnp.log(1.0 + jnp.exp(-jnp.abs(v)))


def _log_sigmoid(v):
    return jnp.minimum(v, 0.0) - jnp.log(1.0 + jnp.exp(-jnp.abs(v)))


def _modulate(x, g, mod):
    ms = jnp.mean(x * x, axis=-1, keepdims=True)
    y = x * lax.rsqrt(ms + NORM_EPS) * g
    return y * (1.0 + mod[:, D:2 * D]) + mod[:, 0:D]


def _dot(a, b):
    return jnp.dot(a, b, preferred_element_type=F32)


def _dot_nt(a, b):
    return lax.dot_general(a, b, (((1,), (1,)), ((), ())), preferred_element_type=F32)


def _const_spec(shape):
    nd = len(shape)
    return pl.BlockSpec(shape, lambda *_: (0,) * nd, pipeline_mode=pl.Buffered(1))


def _mod_spec(per_batch):
    if per_batch:
        return pl.BlockSpec((1, 1, 3 * D), lambda b, i: (b, 0, 0))
    return pl.BlockSpec((1, 1, 3 * D), lambda b, i: (0, 0, 0))


def _adaln_body(cond_ref, w_ref, b_ref, o_ref):
    a = _silu(cond_ref[...]).astype(BF16)
    o_ref[0] = _dot(a, w_ref[0].astype(BF16)) + b_ref[0]


def adaln_all(cond, ada_w, ada_b):
    r = cond.shape[0]
    return pl.pallas_call(
        _adaln_body,
        grid=(DEPTH, 3),
        in_specs=[pl.BlockSpec((r, D), lambda i, j: (0, 0)),
                  pl.BlockSpec((1, D, D), lambda i, j: (i, 0, j)),
                  pl.BlockSpec((1, 1, D), lambda i, j: (i, 0, j))],
        out_specs=pl.BlockSpec((1, r, D), lambda i, j: (i, 0, j)),
        out_shape=jax.ShapeDtypeStruct((DEPTH, r, 3 * D), F32),
        compiler_params=_cparams("arbitrary", "arbitrary"),
        name="adaln",
    )(cond, ada_w, ada_b.reshape(DEPTH, 1, 3 * D))


def _halo_specs(tm, width, seq):
    r = tm // HALO
    last = seq // HALO - 1
    return [pl.BlockSpec((1, tm, width), lambda b, i: (b, i, 0)),
            pl.BlockSpec((1, HALO, width), lambda b, i: (b, jnp.maximum(i * r - 1, 0), 0)),
            pl.BlockSpec((1, HALO, width), lambda b, i: (b, jnp.minimum((i + 1) * r, last), 0))]


def _halo_rows(x_ref, xp_ref, xn_ref, g, mod):
    i = pl.program_id(1)
    h = _modulate(x_ref[0], g, mod)
    hp = jnp.where(i == 0, 0.0, _modulate(xp_ref[0], g, mod))
    hn = jnp.where(i == pl.num_programs(1) - 1, 0.0, _modulate(xn_ref[0], g, mod))
    return h, jnp.concatenate([hp, h, hn], axis=0).astype(BF16)


def _conv_silu(e, tm, w, b):
    n = e.shape[0]
    keep = slice(HALO, HALO + tm)
    y = (w[0:1, :] * pltpu.roll(e, 1, 0)[keep, :] + w[1:2, :] * e[keep, :]
         + w[2:3, :] * pltpu.roll(e, n - 1, 0)[keep, :] + w[3:4, :] * pltpu.roll(e, n - 2, 0)[keep, :] + b)
    return _silu(y)


def _rms(v, g):
    return v * lax.rsqrt(jnp.mean(v * v, axis=-1, keepdims=True) + NORM_EPS) * g


def _outproj_body(*refs, kind, final_norm):
    if kind == "mla":
        o_ref, z_ref, x_ref, mod_ref, w_ref = refs[:5]
        rest = refs[5:]
        o = jnp.concatenate([o_ref[0, k] for k in range(o_ref.shape[1])], axis=-1)
        pre = o.astype(F32) * _silu(z_ref[0].astype(F32))
    elif kind == "ssd":
        yf_ref, yb_ref, xs_ref, z_ref, x_ref, mod_ref, w_ref, dsk_ref, ng_ref = refs[:9]
        rest = refs[9:]
        y = yf_ref[0, 0].astype(F32) + yb_ref[0, 0].astype(F32) + dsk_ref[...] * xs_ref[0].astype(F32)
        pre = _rms(y * _silu(z_ref[0].astype(F32)), ng_ref[...])
    else:
        hf_ref, hb_ref, op_ref, xc_ref, z_ref, x_ref, mod_ref, w_ref, ng_ref, sk_ref = refs[:10]
        rest = refs[10:]
        hh = (hf_ref[0, 0].astype(F32) + hb_ref[0, 0].astype(F32)) * jax.nn.sigmoid(op_ref[0].astype(F32))
        parts = []
        for h in range(ML_HEADS):
            v = hh[:, h * ML_DH:(h + 1) * ML_DH]
            mu = jnp.mean(v, axis=-1, keepdims=True)
            vc = v - mu
            var = jnp.mean(vc * vc, axis=-1, keepdims=True)
            parts.append(vc * lax.rsqrt(var + NORM_EPS))
        hn = jnp.concatenate(parts, axis=-1) * ng_ref[...]
        pre = (hn + sk_ref[...] * xc_ref[0].astype(F32)) * _silu(z_ref[0].astype(F32))
    out = _dot(pre.astype(BF16), w_ref[...])
    xn = x_ref[0] + mod_ref[0][:, 2 * D:3 * D] * out
    if final_norm:
        fg_ref, xo_ref, yo_ref = rest
        xo_ref[0] = xn
        yo_ref[0] = _rms(xn, fg_ref[...])
    else:
        (xo_ref,) = rest
        xo_ref[0] = xn


def outproj(kind, x, mod, per_batch, w_out, tok_inputs, dir_inputs=(), consts=(), final_g=None):
    bsz, seq, _ = x.shape
    tm = TOKEN_TILE
    args, in_specs = [], []
    for a in dir_inputs:
        wd = a.shape[-1]
        for d in range(2):
            args.append(a)
            in_specs.append(pl.BlockSpec((1, 1, tm, wd), functools.partial(lambda b, i, d: (d, b, i, 0), d=d)))
    for a in tok_inputs:
        args.append(a)
        if a.ndim == 4:
            in_specs.append(pl.BlockSpec((1, a.shape[1], tm, a.shape[-1]), lambda b, i: (b, 0, i, 0)))
        else:
            in_specs.append(pl.BlockSpec((1, tm, a.shape[-1]), lambda b, i: (b, i, 0)))
    args += [x, mod, w_out]
    in_specs += [pl.BlockSpec((1, tm, D), lambda b, i: (b, i, 0)), _mod_spec(per_batch), _const_spec(w_out.shape)]
    for a in consts:
        args.append(a)
        in_specs.append(_const_spec(a.shape))
    tok_spec = pl.BlockSpec((1, tm, D), lambda b, i: (b, i, 0))
    shp = jax.ShapeDtypeStruct((bsz, seq, D), F32)
    if final_g is not None:
        args.append(final_g.reshape(1, D))
        in_specs.append(_const_spec((1, D)))
        out_specs, out_shape = [tok_spec, tok_spec], [shp, shp]
    else:
        out_specs, out_shape = [tok_spec], [shp]
    res = pl.pallas_call(
        functools.partial(_outproj_body, kind=kind, final_norm=final_g is not None),
        grid=(bsz, seq // tm),
        in_specs=in_specs, out_specs=out_specs, out_shape=out_shape,
        compiler_params=_cparams("parallel", "parallel"),
        name="outproj_" + kind,
    )(*args)
    return res if final_g is not None else res[0]


def _mla_pre_body(x_ref, g_ref, mod_ref, wcq_ref, wckv_ref, wkpe_ref, wkpes_ref, wz_ref, qg_ref, kvg_ref,
                  wuq1_ref, wuq2_ref, cq_ref, sq_ref, ck_ref, sk_ref,
                  q_ref, ckv_ref, kraw_ref, krope_ref, z_ref):
    h = _modulate(x_ref[0], g_ref[...], mod_ref[0]).astype(BF16)
    z_ref[0] = _dot(h, wz_ref[...]).astype(BF16)
    ckv_ref[0] = _rms(_dot(h, wckv_ref[...]), kvg_ref[...])
    kraw = _dot(h, wkpe_ref[...])
    kraw_ref[0] = kraw
    krope_ref[0] = kraw * ck_ref[...] + _dot(h, wkpes_ref[...]) * sk_ref[...]
    cq = _rms(_dot(h, wcq_ref[...]), qg_ref[...]).astype(BF16)
    cosq = cq_ref[...]
    sinq = sq_ref[...]
    for c0 in range(0, MLA_HEADS * LANES, 512):
        a = _dot(cq, wuq1_ref[:, c0:c0 + 512])
        b = _dot(cq, wuq2_ref[:, c0:c0 + 512])
        for s in range(0, 512, LANES):
            hd = (c0 + s) // LANES
            q_ref[0, hd // 2, :, (hd % 2) * LANES:(hd % 2 + 1) * LANES] = (
                a[:, s:s + LANES] * cosq + b[:, s:s + LANES] * sinq).astype(BF16)


def _head_slabs(w, off=0):
    k, nh, dh = w.shape
    return jnp.pad(w, ((0, 0), (0, 0), (off, LANES - off - dh))).reshape(k, nh * LANES)


def _swap_halves(w):
    half = w.shape[-1] // 2
    return jnp.concatenate([w[..., half:], w[..., :half]], axis=-1)


def _rope_tables(seq, use_rope):
    zeros32 = jnp.zeros((seq, LANES - MLA_NOPE - MLA_ROPE), F32)
    if use_rope:
        rows = seq // GRID_W
        row = jnp.repeat(jnp.arange(rows), GRID_W).astype(F32)
        col = jnp.tile(jnp.arange(GRID_W), rows).astype(F32)
        half = MLA_ROPE // 2
        inv = 1.0 / (ROPE_BASE ** (jnp.arange(0, half, 2, dtype=F32) / half))
        ang = jnp.concatenate([row[:, None] * inv, col[:, None] * inv], axis=-1)
        cos, sin = jnp.cos(ang), jnp.sin(ang)
    else:
        cos = jnp.ones((seq, MLA_ROPE // 2), F32)
        sin = jnp.zeros((seq, MLA_ROPE // 2), F32)
    ones64 = jnp.ones((seq, MLA_NOPE), F32)
    zeros64 = jnp.zeros((seq, MLA_NOPE), F32)
    cos_k = jnp.concatenate([zeros64, cos, cos, zeros32], axis=-1)
    sin_k = jnp.concatenate([zeros64, -sin, sin, zeros32], axis=-1)
    q_scale = MLA_SCALE * LOG2_E
    cos_q = jnp.concatenate([ones64, cos, cos, zeros32], axis=-1) * q_scale
    sin_q = sin_k * q_scale
    return cos_q, sin_q, cos_k, sin_k


def mla_pre(x, g, mod, per_batch, wts, tables):
    bsz, seq, _ = x.shape
    tm = TOKEN_TILE
    tok = lambda w, dt: (pl.BlockSpec((1, tm, w), lambda b, i: (b, i, 0)), jax.ShapeDtypeStruct((bsz, seq, w), dt))
    q_out = (pl.BlockSpec((1, MLA_HEADS // 2, tm, 2 * LANES), lambda b, i: (b, 0, i, 0)),
             jax.ShapeDtypeStruct((bsz, MLA_HEADS // 2, seq, 2 * LANES), BF16))
    outs = [q_out, tok(MLA_KV_RANK, F32), tok(LANES, F32), tok(LANES, F32), tok(D, BF16)]
    tab_spec = pl.BlockSpec((tm, LANES), lambda b, i: (i, 0))
    in_specs = ([pl.BlockSpec((1, tm, D), lambda b, i: (b, i, 0)), _const_spec((1, D)), _mod_spec(per_batch)]
                + [_const_spec(w.shape) for w in wts] + [tab_spec] * 4)
    return pl.pallas_call(
        _mla_pre_body,
        grid=(bsz, seq // tm),
        in_specs=in_specs,
        out_specs=[o[0] for o in outs], out_shape=[o[1] for o in outs],
        compiler_params=_cparams("parallel", "parallel"),
        name="mla_pre",
    )(x, g.reshape(1, D), mod, *wts, *tables)


def _kv_expand_body(ckv_ref, kpe_ref, wkt_ref, wv_ref, kt_ref, v_ref):
    c = ckv_ref[0].astype(BF16)
    v_ref[0] = _dot(c, wv_ref[...]).astype(BF16)
    kpe_t = kpe_ref[0].T
    for c0 in range(0, MLA_HEADS * LANES, 512):
        a = _dot_nt(wkt_ref[c0:c0 + 512, :], c)
        for s in range(0, 512, LANES):
            kt_ref[0, 0, c0 + s:c0 + s + LANES, :] = (a[s:s + LANES, :] + kpe_t).astype(BF16)


def kv_expand(ckv, kpe_slab, wkt_aug, wv, tk):
    bsz, seq, _ = ckv.shape
    return pl.pallas_call(
        _kv_expand_body,
        grid=(bsz, seq // tk),
        in_specs=[pl.BlockSpec((1, tk, MLA_KV_RANK), lambda b, i: (b, i, 0)),
                  pl.BlockSpec((1, tk, LANES), lambda b, i: (b, i, 0)),
                  _const_spec(wkt_aug.shape), _const_spec(wv.shape)],
        out_specs=[pl.BlockSpec((1, 1, MLA_HEADS * LANES, tk), lambda b, i: (b, i, 0, 0)),
                   pl.BlockSpec((1, tk, MLA_HEADS * MLA_V), lambda b, i: (b, i, 0))],
        out_shape=[jax.ShapeDtypeStruct((bsz, seq // tk, MLA_HEADS * LANES, tk), BF16),
                   jax.ShapeDtypeStruct((bsz, seq, MLA_HEADS * MLA_V), BF16)],
        compiler_params=_cparams("parallel", "parallel"),
        name="mla_kv_expand",
    )(ckv, kpe_slab, wkt_aug, wv)


ATTN_ROWS = 16


def _attn_body(q_ref, kt_ref, v_ref, o_ref, s_ref, p_ref, m_ref, l_ref, a_ref, acc_ref, *, nk):
    tq = q_ref.shape[2]
    tk = kt_ref.shape[3]
    rb = ATTN_ROWS
    ncol = tk // LANES
    left = lax.broadcasted_iota(jnp.int32, (tq, LANES), 1) < MLA_V

    def scores(kc):
        for hh in range(2):
            s_ref[kc % 2, hh] = _dot(q_ref[0, 0, :, hh * LANES:(hh + 1) * LANES],
                                     kt_ref[0, kc, hh * LANES:(hh + 1) * LANES, :])

    def softmax(kc):
        sl = kc % 2
        for hh in range(2):
            for r in range(0, tq, rb):
                mx = s_ref[sl, hh, r:r + rb, 0:LANES]
                for cc in range(1, ncol):
                    mx = jnp.maximum(mx, s_ref[sl, hh, r:r + rb, cc * LANES:(cc + 1) * LANES])
                mx = jnp.broadcast_to(jnp.max(mx, axis=-1, keepdims=True), (rb, LANES))
                if kc > 0:
                    m_old = m_ref[hh, r:r + rb, :]
                    mx = jnp.maximum(m_old, mx)
                    a_ref[hh, r:r + rb, :] = jnp.exp2(m_old - mx)
                m_ref[hh, r:r + rb, :] = mx
            for r in range(0, tq, rb):
                m_new = m_ref[hh, r:r + rb, :]
                tot = None
                for cc in range(ncol):
                    p = jnp.exp2(s_ref[sl, hh, r:r + rb, cc * LANES:(cc + 1) * LANES] - m_new)
                    p_ref[hh, r:r + rb, cc * LANES:(cc + 1) * LANES] = p.astype(BF16)
                    tot = p if tot is None else tot + p
                tot = jnp.broadcast_to(jnp.sum(tot, axis=-1, keepdims=True), (rb, LANES))
                if kc > 0:
                    tot = a_ref[hh, r:r + rb, :] * l_ref[hh, r:r + rb, :] + tot
                l_ref[hh, r:r + rb, :] = tot

    def values(kc):
        vv = v_ref[0, kc * tk:(kc + 1) * tk, :]
        pv = jnp.where(left, _dot(p_ref[0], vv), _dot(p_ref[1], vv))
        if kc > 0:
            pv = acc_ref[...] * jnp.where(left, a_ref[0], a_ref[1]) + pv
        acc_ref[...] = pv

    scores(0)
    for kc in range(nk):
        if kc + 1 < nk:
            scores(kc + 1)
        softmax(kc)
        values(kc)
    o_ref[0, 0] = (acc_ref[...] / jnp.where(left, l_ref[0], l_ref[1])).astype(BF16)


def mla_attention(q_aug, kt, v):
    bsz, _, seq, _ = q_aug.shape
    _, nk, _, tk = kt.shape
    lk = nk * tk
    tq = min(ATTN_TQ, seq)
    stat = pltpu.VMEM((2, tq, LANES), F32)
    return pl.pallas_call(
        functools.partial(_attn_body, nk=nk),
        grid=(bsz, MLA_HEADS // 2, seq // tq),
        in_specs=[pl.BlockSpec((1, 1, tq, 2 * LANES), lambda b, h, i: (b, h, i, 0)),
                  pl.BlockSpec((1, nk, 2 * LANES, tk), lambda b, h, i: (b, 0, h, 0)),
                  pl.BlockSpec((1, lk, LANES), lambda b, h, i: (b, 0, h))],
        out_specs=pl.BlockSpec((1, 1, tq, LANES), lambda b, h, i: (b, h, i, 0)),
        out_shape=jax.ShapeDtypeStruct((bsz, MLA_HEADS // 2, seq, LANES), BF16),
        scratch_shapes=[pltpu.VMEM((2, 2, tq, tk), F32), pltpu.VMEM((2, tq, tk), BF16), stat, stat, stat,
                        pltpu.VMEM((tq, LANES), F32)],
        compiler_params=_cparams("parallel", "parallel", "arbitrary"),
        name="mla_attention",
    )(q_aug, kt, v)


def _mla_weights(w_in, q_norm_g, kv_norm_g, w_uq, w_uk, w_uv):
    o1, o2, o3 = MLA_Q_RANK, MLA_Q_RANK + MLA_KV_RANK, MLA_Q_RANK + MLA_KV_RANK + MLA_ROPE
    w_kpe = w_in[:, o2:o3]
    pad_k = lambda w: jnp.pad(w, ((0, 0), (MLA_NOPE, LANES - MLA_NOPE - MLA_ROPE)))
    uq_pe = w_uq[..., MLA_NOPE:]
    wuq1 = _head_slabs(w_uq)
    wuq2 = _head_slabs(_swap_halves(uq_pe), off=MLA_NOPE)
    pre = (w_in[:, :o1].astype(BF16), w_in[:, o1:o2].astype(BF16), pad_k(w_kpe).astype(BF16),
           pad_k(_swap_halves(w_kpe)).astype(BF16), w_in[:, o3:].astype(BF16),
           q_norm_g.reshape(1, -1), kv_norm_g.reshape(1, -1), wuq1.astype(BF16), wuq2.astype(BF16))
    wkt_aug = _head_slabs(w_uk).T.astype(BF16)
    wv = w_uv.reshape(MLA_KV_RANK, MLA_HEADS * MLA_V).astype(BF16)
    return pre, wkt_aug, wv


def _key_chunk(lk):
    tk = min(ATTN_TK, lk)
    while lk % tk:
        tk -= LANES
    return tk


def mla_layer(x, g, mod, per_batch, wts, w_out, use_rope, ctx=None, final_g=None):
    pre_w, wkt_aug, wv = wts
    seq = x.shape[1]
    q_aug, ckv, kraw, krope, z = mla_pre(x, g, mod, per_batch, pre_w, _rope_tables(seq, use_rope))
    ckv_all, kpe_all = ckv, krope
    if ctx is not None:
        ctx_ckv, ctx_kpe = ctx
        ckv_all = jnp.concatenate([ckv, ctx_ckv], axis=1)
        kpe_all = jnp.concatenate(
            [krope, jnp.pad(ctx_kpe, ((0, 0), (0, 0), (MLA_NOPE, LANES - MLA_NOPE - MLA_ROPE)))], axis=1)
    kt, v = kv_expand(ckv_all, kpe_all, wkt_aug, wv, _key_chunk(ckv_all.shape[1]))
    o = mla_attention(q_aug, kt, v)
    res = outproj("mla", x, mod, per_batch, w_out, [o, z], final_g=final_g)
    return res, ckv, kraw[..., MLA_NOPE:MLA_NOPE + MLA_ROPE]


DT_REP = 4 * SSD_HEADS


def _ssd_in_body(x_ref, xp_ref, xn_ref, g_ref, mod_ref, wz_ref, wx_ref, wdt_ref, bdt_ref, wdtt_ref, bdtt_ref,
                 cw_ref, cb_ref, z_ref, xs_ref, bt_ref, c_ref, dt_ref, dtt_ref):
    tm = x_ref.shape[1]
    h, h_ext = _halo_rows(x_ref, xp_ref, xn_ref, g_ref[...], mod_ref[0])
    hb = h.astype(BF16)
    for c0 in range(0, SSD_INNER, 1024):
        z_ref[0, :, c0:c0 + 1024] = _dot(hb, wz_ref[:, c0:c0 + 1024]).astype(BF16)
    dt_ref[0] = _softplus(_dot(hb, wdt_ref[...]) + bdt_ref[...])
    dtt_ref[0] = _softplus(_dot_nt(wdtt_ref[...], hb) + bdtt_ref[...])
    cw = CONV_COLS
    for c0 in range(0, SSD_CONV_DIM, cw):
        e = _dot(h_ext, wx_ref[:, c0:c0 + cw])
        y = _conv_silu(e, tm, cw_ref[:, c0:c0 + cw], cb_ref[:, c0:c0 + cw])
        if c0 < SSD_INNER:
            xs_ref[0, :, c0:c0 + cw] = y.astype(BF16)
        elif c0 < SSD_INNER + SSD_BC:
            o = c0 - SSD_INNER
            y_t = y.T.astype(BF16)
            for ch in range(tm // SSD_CHUNK):
                bt_ref[0, ch, o:o + cw, :] = y_t[:, ch * SSD_CHUNK:(ch + 1) * SSD_CHUNK]
        else:
            o = c0 - SSD_INNER - SSD_BC
            c_ref[0, :, o:o + cw] = y.astype(BF16)


def ssd_in(x, g, mod, per_batch, operands):
    bsz, seq, _ = x.shape
    tm = TOKEN_TILE
    tok = lambda w: pl.BlockSpec((1, tm, w), lambda b, i: (b, i, 0))
    return pl.pallas_call(
        _ssd_in_body,
        grid=(bsz, seq // tm),
        in_specs=_halo_specs(tm, D, seq) + [_const_spec((1, D)), _mod_spec(per_batch)]
        + [_const_spec(op.shape) for op in operands],
        out_specs=[tok(SSD_INNER), tok(SSD_INNER),
                   pl.BlockSpec((1, tm // SSD_CHUNK, SSD_BC, SSD_CHUNK), lambda b, i: (b, i, 0, 0)),
                   tok(SSD_BC), tok(2 * DT_REP), pl.BlockSpec((1, 2 * SSD_HEADS, tm), lambda b, i: (b, 0, i))],
        out_shape=[jax.ShapeDtypeStruct((bsz, seq, SSD_INNER), BF16),
                   jax.ShapeDtypeStruct((bsz, seq, SSD_INNER), BF16),
                   jax.ShapeDtypeStruct((bsz, seq // SSD_CHUNK, SSD_BC, SSD_CHUNK), BF16),
                   jax.ShapeDtypeStruct((bsz, seq, SSD_BC), BF16),
                   jax.ShapeDtypeStruct((bsz, seq, 2 * DT_REP), F32),
                   jax.ShapeDtypeStruct((bsz, 2 * SSD_HEADS, seq), F32)],
        compiler_params=_cparams("parallel", "parallel"),
        name="ssd_in",
    )(x, x, x, g.reshape(1, D), mod, *operands)


def _scan_masks(d, n):
    ri = lax.broadcasted_iota(jnp.int32, (n, n), 0)
    ci = lax.broadcasted_iota(jnp.int32, (n, n), 1)
    diff = (ri - ci) * (1 - 2 * d)
    return diff >= 0, diff <= 0


def _split3(v):
    lane = lax.broadcasted_iota(jnp.int32, v.shape, 1)
    hi = v.astype(BF16).astype(F32)
    r1 = v - hi
    mid = r1.astype(BF16).astype(F32)
    lo = r1 - mid
    w = SSD_HEADS
    return jnp.where(lane < w, hi, jnp.where(lane < 2 * w, mid, jnp.where(lane < 3 * w, lo, 0.0))).astype(BF16)


def _expand_matrix(width):
    k = jnp.arange(DT_REP)[:, None]
    col_head = jnp.arange(SSD_HEADS * width)[None, :] // width
    return ((k < 3 * SSD_HEADS) & (k % SSD_HEADS == col_head)).astype(BF16)


def _ssd_scan_body(*refs, has_h0):
    if has_h0:
        xs_ref, bt_ref, c_ref, dt_ref, dtt_ref, a_ref, at_ref, eq_ref, ep_ref, h0_ref, y_ref, hout_ref, ht_s = refs
    else:
        xs_ref, bt_ref, c_ref, dt_ref, dtt_ref, a_ref, at_ref, eq_ref, ep_ref, y_ref, hout_ref, ht_s = refs
    d = pl.program_id(1)
    c = pl.program_id(2)
    q = SSD_CHUNK
    hp_n = SSD_HEADS * SSD_HEADDIM

    @pl.when(c == 0)
    def _():
        if has_h0:
            ht_s[...] = h0_ref[0, 0].reshape(hp_n, SSD_STATE).T
        else:
            ht_s[...] = jnp.zeros_like(ht_s)

    mask, mask_t = _scan_masks(d, q)
    dt = dt_ref[0]
    dtt = dtt_ref[0]
    da = dt * a_ref[0]
    dat = dtt * at_ref[0]
    cum = jnp.dot(mask.astype(F32), da, precision=HIGHEST, preferred_element_type=F32)
    cum_t = jnp.dot(dat, mask_t.astype(F32), precision=HIGHEST, preferred_element_type=F32)
    tot = jnp.sum(da, axis=0, keepdims=True)
    cum3 = _split3(cum)
    ecum3 = _split3(jnp.exp(cum))
    wend3 = _split3(jnp.exp(tot - cum) * dt)
    left = lax.broadcasted_iota(jnp.int32, (q, LANES), 1) < SSD_HEADDIM
    heads_g = SSD_HEADS // SSD_GROUPS
    gw = heads_g * SSD_HEADDIM

    for g in range(SSD_GROUPS):
        cum_b = _dot(cum3, eq_ref[:, g * heads_g * q:(g + 1) * heads_g * q])
        ecum_b = _dot(ecum3, ep_ref[:, g * gw:(g + 1) * gw])
        wend_b = _dot(wend3, ep_ref[:, g * gw:(g + 1) * gw])
        decay_b = jnp.where(d == 0, ecum_b[q - 1:q, :], ecum_b[0:1, :])
        cg = c_ref[0, :, g * SSD_STATE:(g + 1) * SSD_STATE]
        btg = bt_ref[0, 0, g * SSD_STATE:(g + 1) * SSD_STATE, :]
        cb = _dot(cg, btg)
        y_state = _dot(cg, ht_s[:, g * gw:(g + 1) * gw].astype(BF16))
        for pr in range(gw // LANES):
            lo = g * gw + pr * LANES
            sl = slice(pr * LANES, (pr + 1) * LANES)
            xs_b = xs_ref[0, :, lo:lo + LANES]
            xs = xs_b.astype(F32)
            ys = []
            for hh in range(2):
                hl = 2 * pr + hh
                h = g * heads_g + hl
                seg = cum_b[:, hl * q:(hl + 1) * q] - cum_t[h:h + 1, :]
                lm = jnp.exp(jnp.where(mask, seg, -jnp.inf))
                mm = (cb * lm * dtt[h:h + 1, :]).astype(BF16)
                ys.append(_dot(mm, xs_b))
            y_ref[0, 0, :, lo:lo + LANES] = (jnp.where(left, ys[0], ys[1])
                                             + y_state[:, sl] * ecum_b[:, sl]).astype(BF16)
            xw = (xs * wend_b[:, sl]).astype(BF16)
            ht_s[:, lo:lo + LANES] = ht_s[:, lo:lo + LANES] * decay_b[:, sl] + _dot(btg, xw)

    @pl.when(c == pl.num_programs(2) - 1)
    def _():
        hout_ref[0, 0] = ht_s[...].T.reshape(SSD_HEADS, SSD_HEADDIM, SSD_STATE)


def ssd_scan(xs, bt, cm, dt_rep, dtt, a, h0):
    bsz, seq, _ = xs.shape
    q = SSD_CHUNK
    nc = seq // q
    cidx = lambda d, c: c + d * (nc - 1 - 2 * c)
    a_rep = jnp.concatenate([a, a, a, jnp.zeros_like(a)], axis=-1).reshape(2, 1, DT_REP)
    eq = _expand_matrix(q)
    ep = _expand_matrix(SSD_HEADDIM)
    in_specs = [pl.BlockSpec((1, q, SSD_INNER), lambda b, d, c: (b, cidx(d, c), 0)),
                pl.BlockSpec((1, 1, SSD_BC, q), lambda b, d, c: (b, cidx(d, c), 0, 0)),
                pl.BlockSpec((1, q, SSD_BC), lambda b, d, c: (b, cidx(d, c), 0)),
                pl.BlockSpec((1, q, DT_REP), lambda b, d, c: (b, cidx(d, c), d)),
                pl.BlockSpec((1, SSD_HEADS, q), lambda b, d, c: (b, d, cidx(d, c))),
                pl.BlockSpec((1, 1, DT_REP), lambda b, d, c: (d, 0, 0)),
                pl.BlockSpec((1, SSD_HEADS, 1), lambda b, d, c: (d, 0, 0)),
                _const_spec(eq.shape), _const_spec(ep.shape)]
    args = [xs, bt, cm, dt_rep, dtt, a_rep, a.reshape(2, SSD_HEADS, 1), eq, ep]
    st_spec = pl.BlockSpec((1, 1, SSD_HEADS, SSD_HEADDIM, SSD_STATE), lambda b, d, c: (b, d, 0, 0, 0))
    if h0 is not None:
        in_specs.append(st_spec)
        args.append(h0)
    return pl.pallas_call(
        functools.partial(_ssd_scan_body, has_h0=h0 is not None),
        grid=(bsz, 2, nc),
        in_specs=in_specs,
        out_specs=[pl.BlockSpec((1, 1, q, SSD_INNER), lambda b, d, c: (d, b, cidx(d, c), 0)), st_spec],
        out_shape=[jax.ShapeDtypeStruct((2, bsz, seq, SSD_INNER), BF16),
                   jax.ShapeDtypeStruct((bsz, 2, SSD_HEADS, SSD_HEADDIM, SSD_STATE), F32)],
        scratch_shapes=[pltpu.VMEM((SSD_STATE, SSD_INNER), F32)],
        compiler_params=_cparams("parallel", "arbitrary", "arbitrary"),
        name="ssd_scan",
    )(*args)


def _ssd_weights(w_in, conv_w, conv_b, dt_bias, a_log, d_skip, norm_g, w_out):
    o1, o2 = SSD_INNER, SSD_INNER + SSD_CONV_DIM
    w_dt = w_in[:, o2:]
    nh = SSD_HEADS
    rep = lambda t: jnp.concatenate([t[..., :nh]] * 3 + [jnp.zeros_like(t[..., :nh])]
                                    + [t[..., nh:]] * 3 + [jnp.zeros_like(t[..., :nh])], axis=-1)
    operands = (w_in[:, :o1].astype(BF16), w_in[:, o1:o2].astype(BF16),
                rep(w_dt).astype(BF16), rep(dt_bias.reshape(1, -1)),
                w_dt.T.astype(BF16), dt_bias.reshape(-1, 1),
                conv_w, conv_b.reshape(1, SSD_CONV_DIM))
    a = -jnp.exp(a_log.astype(F32))
    dsk = jnp.repeat(d_skip, SSD_HEADDIM).reshape(1, SSD_INNER)
    return operands, a, dsk, norm_g.reshape(1, SSD_INNER), w_out.astype(BF16)


def ssd_layer(x, g, mod, per_batch, wts, h0):
    operands, a, dsk, norm_g, w_out = wts
    z, xs, bt, cm, dt_rep, dtt = ssd_in(x, g, mod, per_batch, operands)
    y, h_new = ssd_scan(xs, bt, cm, dt_rep, dtt, a, h0)
    xn = outproj("ssd", x, mod, per_batch, w_out, [xs, z], dir_inputs=[y], consts=[dsk, norm_g])
    return xn, h_new


def _ml_gate_act(r, transposed):
    if transposed:
        idx = lax.broadcasted_iota(jnp.int32, r.shape, 0)
    else:
        idx = lax.broadcasted_iota(jnp.int32, r.shape, 1) % LANES
    return jnp.where((idx // ML_HEADS) % 2 == 1, _log_sigmoid(r), r)


def _ml_in_body(x_ref, xp_ref, xn_ref, g_ref, mod_ref, wxm_ref, wo_ref, wz_ref, wg_ref, bg_ref, wgt_ref, bgt_ref,
                cw_ref, cb_ref, wq_ref, wkt_ref, wv_ref,
                xc_ref, q_ref, kt_ref, v_ref, o_ref, z_ref, gate_ref, gatet_ref):
    tm = x_ref.shape[1]
    h, h_ext = _halo_rows(x_ref, xp_ref, xn_ref, g_ref[...], mod_ref[0])
    hb = h.astype(BF16)
    for c0 in range(0, ML_INNER, 1024):
        o_ref[0, :, c0:c0 + 1024] = _dot(hb, wo_ref[:, c0:c0 + 1024]).astype(BF16)
        z_ref[0, :, c0:c0 + 1024] = _dot(hb, wz_ref[:, c0:c0 + 1024]).astype(BF16)
    gate_ref[0] = _ml_gate_act(_dot(hb, wg_ref[...]) + bg_ref[...], False)
    gatet_ref[0] = _ml_gate_act(_dot_nt(wgt_ref[...], hb) + bgt_ref[...], True)
    for hd in range(ML_HEADS):
        sl = slice(hd * ML_DH, (hd + 1) * ML_DH)
        e = _dot(h_ext, wxm_ref[:, sl])
        xm = e[HALO:tm + HALO, :]
        xc_b = _conv_silu(e, tm, cw_ref[:, sl], cb_ref[:, sl]).astype(BF16)
        xc_ref[0, :, sl] = xc_b
        q_ref[0, :, sl] = _dot(xc_b, wq_ref[hd]).astype(BF16)
        kt_ref[0, 0, hd] = (_dot_nt(wkt_ref[hd], xc_b) * (ML_DH ** -0.5)).astype(BF16)
        v_ref[0, :, sl] = _dot(xm.astype(BF16), wv_ref[hd]).astype(BF16)


def ml_in(x, g, mod, per_batch, operands):
    bsz, seq, _ = x.shape
    tm = TOKEN_TILE
    tok = pl.BlockSpec((1, tm, ML_INNER), lambda b, i: (b, i, 0))
    big = lambda dt: jax.ShapeDtypeStruct((bsz, seq, ML_INNER), dt)
    ng = 4 * ML_HEADS
    return pl.pallas_call(
        _ml_in_body,
        grid=(bsz, seq // tm),
        in_specs=_halo_specs(tm, D, seq) + [_const_spec((1, D)), _mod_spec(per_batch)]
        + [_const_spec(op.shape) for op in operands],
        out_specs=[tok, tok, pl.BlockSpec((1, 1, ML_HEADS, ML_DH, tm), lambda b, i: (b, i, 0, 0, 0)), tok, tok, tok,
                   pl.BlockSpec((1, tm, 2 * LANES), lambda b, i: (b, i, 0)),
                   pl.BlockSpec((1, ng, tm), lambda b, i: (b, 0, i))],
        out_shape=[big(BF16), big(BF16), jax.ShapeDtypeStruct((bsz, seq // tm, ML_HEADS, ML_DH, tm), BF16), big(BF16),
                   big(BF16), big(BF16), jax.ShapeDtypeStruct((bsz, seq, 2 * LANES), F32),
                   jax.ShapeDtypeStruct((bsz, ng, seq), F32)],
        compiler_params=_cparams("parallel", "parallel"),
        name="ml_in",
    )(x, x, x, g.reshape(1, D), mod, *operands)


def _ml_scan_body(*refs, has_state):
    if has_state:
        q_ref, kt_ref, v_ref, g_ref, gt_ref, c0_ref, n0_ref, m0_ref, h_ref, c_ref, n_ref, m_ref = refs
    else:
        q_ref, kt_ref, v_ref, g_ref, gt_ref, h_ref, c_ref, n_ref, m_ref = refs
    d = pl.program_id(1)
    c = pl.program_id(2)
    lc = q_ref.shape[1]
    nh = ML_HEADS

    @pl.when(c == 0)
    def _():
        if has_state:
            c_ref[...] = c0_ref[...]
            n_ref[...] = n0_ref[...]
            m_ref[...] = m0_ref[...]
        else:
            c_ref[...] = jnp.zeros_like(c_ref)
            n_ref[...] = jnp.zeros_like(n_ref)
            m_ref[...] = jnp.zeros_like(m_ref)

    mask, mask_t = _scan_masks(d, lc)
    fwd = d == 0
    gates = g_ref[0]
    gates_t = gt_ref[0]
    lf = gates[:, nh:2 * nh]
    lf_t = gates_t[nh:2 * nh, :]
    cum = jnp.dot(mask.astype(F32), lf, precision=HIGHEST, preferred_element_type=F32)
    cum_t = jnp.dot(lf_t, mask_t.astype(F32), precision=HIGHEST, preferred_element_type=F32)
    tot = jnp.sum(lf, axis=0, keepdims=True)
    for h in range(nh):
        sl = slice(h * ML_DH, (h + 1) * ML_DH)
        bcol = cum[:, h:h + 1]
        brow = cum_t[h:h + 1, :]
        irow = gates_t[h:h + 1, :]
        m_prev = m_ref[0, 0, h:h + 1, 0:1]
        dm = jnp.where(mask, bcol - brow + irow, -jnp.inf)
        m_inter = bcol + m_prev
        m_i = jnp.maximum(jnp.max(dm, axis=-1, keepdims=True), m_inter)
        wmat = jnp.exp(dm - m_i)
        g_inter = jnp.exp(m_inter - m_i)
        qh = q_ref[0, :, sl]
        kth = kt_ref[0, 0, h]
        vh = v_ref[0, :, sl]
        s = _dot(qh, kth) * wmat
        c_old = c_ref[0, 0, h]
        n_old = n_ref[0, 0, h:h + 1, :]
        num = _dot(s.astype(BF16), vh) + g_inter * _dot(qh, c_old.astype(BF16))
        qn = _dot_nt(qh, jnp.broadcast_to(n_old, (SUBLANES, ML_DH)).astype(BF16))[:, 0:1]
        den = jnp.sum(s, axis=-1, keepdims=True) + g_inter * qn
        denom = jnp.maximum(jnp.abs(den), jnp.exp(-m_i))
        h_ref[0, 0, :, sl] = (num / denom).astype(BF16)
        m_new = jnp.where(fwd, m_i[lc - 1:lc, :], m_i[0:1, :])
        tot_h = tot[:, h:h + 1]
        w_row = jnp.exp(tot_h - brow + irow - m_new)
        decay = jnp.exp(tot_h + m_prev - m_new)
        kw = (kth.astype(F32) * w_row).astype(BF16)
        c_ref[0, 0, h] = decay * c_old + _dot(kw, vh)
        w8 = jnp.broadcast_to(w_row, (SUBLANES, lc)).astype(BF16)
        n_ref[0, 0, h:h + 1, :] = decay * n_old + _dot_nt(w8, kth)[0:1, :]
        m_ref[0, 0, h:h + 1, :] = jnp.broadcast_to(m_new, (1, LANES))


def ml_scan(q, kt, v, gates, gates_t, state):
    bsz, seq, _ = q.shape
    lc = kt.shape[-1]
    nc = seq // lc
    cidx = lambda d, c: c + d * (nc - 1 - 2 * c)
    tok = pl.BlockSpec((1, lc, ML_INNER), lambda b, d, c: (b, cidx(d, c), 0))
    in_specs = [tok,
                pl.BlockSpec((1, 1, ML_HEADS, ML_DH, lc), lambda b, d, c: (b, cidx(d, c), 0, 0, 0)),
                tok,
                pl.BlockSpec((1, lc, LANES), lambda b, d, c: (b, cidx(d, c), d)),
                pl.BlockSpec((1, 2 * ML_HEADS, lc), lambda b, d, c: (b, d, cidx(d, c)))]
    args = [q, kt, v, gates, gates_t]
    st_specs = [pl.BlockSpec((1, 1, ML_HEADS, ML_DH, ML_DH), lambda b, d, c: (b, d, 0, 0, 0)),
                pl.BlockSpec((1, 1, ML_HEADS, ML_DH), lambda b, d, c: (b, d, 0, 0)),
                pl.BlockSpec((1, 1, SUBLANES, LANES), lambda b, d, c: (b, d, 0, 0))]
    if state is not None:
        in_specs += st_specs
        args += list(state)
    return pl.pallas_call(
        functools.partial(_ml_scan_body, has_state=state is not None),
        grid=(bsz, 2, nc),
        in_specs=in_specs,
        out_specs=[pl.BlockSpec((1, 1, lc, ML_INNER), lambda b, d, c: (d, b, cidx(d, c), 0))] + st_specs,
        out_shape=[jax.ShapeDtypeStruct((2, bsz, seq, ML_INNER), BF16),
                   jax.ShapeDtypeStruct((bsz, 2, ML_HEADS, ML_DH, ML_DH), F32),
                   jax.ShapeDtypeStruct((bsz, 2, ML_HEADS, ML_DH), F32),
                   jax.ShapeDtypeStruct((bsz, 2, SUBLANES, LANES), F32)],
        compiler_params=_cparams("parallel", "arbitrary", "arbitrary"),
        name="ml_scan",
    )(*args)


def _ml_weights(w_in, gate_b, conv_w, conv_b, w_q, w_k, w_v, norm_g, skip, w_out):
    w_g = w_in[:, 3 * ML_INNER:]
    half = 2 * ML_HEADS
    pad = lambda t: jnp.pad(t, ((0, 0), (0, LANES - half)))
    rep = lambda t: jnp.concatenate([pad(t[:, :half]), pad(t[:, half:])], axis=-1)
    operands = (w_in[:, :ML_INNER].astype(BF16), w_in[:, ML_INNER:2 * ML_INNER].astype(BF16),
                w_in[:, 2 * ML_INNER:3 * ML_INNER].astype(BF16),
                rep(w_g).astype(BF16), rep(gate_b.reshape(1, -1)), w_g.T.astype(BF16), gate_b.reshape(-1, 1),
                conv_w, conv_b.reshape(1, ML_INNER),
                w_q.astype(BF16), jnp.swapaxes(w_k, 1, 2).astype(BF16), w_v.astype(BF16))
    return operands, norm_g.reshape(1, ML_INNER), skip.reshape(1, ML_INNER), w_out.astype(BF16)


def ml_layer(x, g, mod, per_batch, wts, state):
    operands, norm_g, skip, w_out = wts
    xc, q, kt, v, o_pre, z, gates, gates_t = ml_in(x, g, mod, per_batch, operands)
    if state is not None:
        c0, n0, m0 = state
        m0 = jnp.broadcast_to(jnp.pad(m0, ((0, 0), (0, 0), (0, SUBLANES - ML_HEADS)))[..., None],
                              m0.shape[:2] + (SUBLANES, LANES))
        state = (c0, n0, m0)
    hs, c_new, n_new, m_new = ml_scan(q, kt, v, gates, gates_t, state)
    xn = outproj("ml", x, mod, per_batch, w_out, [o_pre, xc, z], dir_inputs=[hs], consts=[norm_g, skip])
    return xn, c_new, n_new, m_new[:, :, :ML_HEADS, 0]


def kernel(x_prompt, x_sample, cache_mla_ckv, cache_mla_kpe, state_ssd, state_mlstm_c, state_mlstm_n,
           state_mlstm_m, c, c_ctx, ada_w, ada_b, norm_g, final_norm_g,
           mla_w_in, mla_q_norm_g, mla_kv_norm_g, mla_w_uq, mla_w_uk, mla_w_uv, mla_w_out,
           ssd_w_in, ssd_conv_w, ssd_conv_b, ssd_dt_bias, ssd_a_log, ssd_d, ssd_norm_g, ssd_w_out,
           ml_w_in, ml_gate_b, ml_conv_w, ml_conv_b, ml_w_q, ml_w_k, ml_w_v, ml_norm_g, ml_skip, ml_w_out):
    nb = c.shape[0]
    rows = -(-(nb + 1) // SUBLANES) * SUBLANES
    cond = jnp.zeros((rows, D), F32).at[0].set(c_ctx).at[1:nb + 1].set(c)
    mods = adaln_all(cond, ada_w, ada_b)
    xp, xs = x_prompt, x_sample
    new_ckv, new_kpe, new_ssd, new_c, new_n, new_m = [], [], [], [], [], []
    yp = ys = None
    for i in range(DEPTH):
        kind, j = i % N_MIXERS, i // N_MIXERS
        mod_p = mods[i, 0:1].reshape(1, 1, 3 * D)
        mod_s = mods[i, 1:nb + 1].reshape(nb, 1, 3 * D)
        fg = final_norm_g if i == DEPTH - 1 else None
        if kind == 0:
            wts = _mla_weights(mla_w_in[j], mla_q_norm_g[j], mla_kv_norm_g[j], mla_w_uq[j], mla_w_uk[j], mla_w_uv[j])
            w_out = mla_w_out[j].astype(BF16)
            rp, ckv, kpe = mla_layer(xp, norm_g[i], mod_p, False, wts, w_out, False, final_g=fg)
            rs, _, _ = mla_layer(xs, norm_g[i], mod_s, True, wts, w_out, True,
                                 ctx=(cache_mla_ckv[:, j], cache_mla_kpe[:, j]), final_g=fg)
            new_ckv.append(ckv)
            new_kpe.append(kpe)
        elif kind == 1:
            wts = _ssd_weights(ssd_w_in[j], ssd_conv_w[j], ssd_conv_b[j], ssd_dt_bias[j], ssd_a_log[j], ssd_d[j],
                               ssd_norm_g[j], ssd_w_out[j])
            rp, st = ssd_layer(xp, norm_g[i], mod_p, False, wts, None)
            rs, _ = ssd_layer(xs, norm_g[i], mod_s, True, wts, state_ssd[:, j])
            new_ssd.append(st)
        else:
            wts = _ml_weights(ml_w_in[j], ml_gate_b[j], ml_conv_w[j], ml_conv_b[j], ml_w_q[j], ml_w_k[j], ml_w_v[j],
                              ml_norm_g[j], ml_skip[j], ml_w_out[j])
            rp, cc, nn, mm = ml_layer(xp, norm_g[i], mod_p, False, wts, None)
            rs, _, _, _ = ml_layer(xs, norm_g[i], mod_s, True, wts,
                                   (state_mlstm_c[:, j], state_mlstm_n[:, j], state_mlstm_m[:, j]))
            new_c.append(cc)
            new_n.append(nn)
            new_m.append(mm)
        if fg is not None:
            (xp, yp), (xs, ys) = rp, rs
        else:
            xp, xs = rp, rs
    return (yp, ys, jnp.stack(new_ckv, axis=1), jnp.stack(new_kpe, axis=1), jnp.stack(new_ssd, axis=1),
            jnp.stack(new_c, axis=1), jnp.stack(new_n, axis=1), jnp.stack(new_m, axis=1))
```

```python
import functools
import math

import jax
import jax.numpy as jnp
from jax import lax
from jax.experimental import pallas as pl
from jax.experimental.pallas import tpu as pltpu

F32 = jnp.float32
BF16 = jnp.bfloat16
HIGHEST = lax.Precision.HIGHEST

D = 1024
DEPTH = 4
GRID_W = 64
N_MIXERS = 3
NORM_EPS = 1e-6
CONV_W = 4

MLA_HEADS = 16
MLA_NOPE = 64
MLA_ROPE = 32
MLA_V = 64
MLA_Q_RANK = 384
MLA_KV_RANK = 256
MLA_SCALE = (MLA_NOPE + MLA_ROPE) ** -0.5
LOG2_E = math.log2(math.e)
ROPE_BASE = 10000.0

SSD_INNER = 2 * D
SSD_HEADDIM = 64
SSD_HEADS = SSD_INNER // SSD_HEADDIM
SSD_STATE = 128
SSD_GROUPS = 8
SSD_BC = SSD_GROUPS * SSD_STATE
SSD_CONV_DIM = SSD_INNER + 2 * SSD_BC
SSD_CHUNK = 128

ML_INNER = 2 * D
ML_HEADS = 4
ML_DH = ML_INNER // ML_HEADS

LANES = 128
SUBLANES = 8
VMEM_LIMIT = 56 * 1024 * 1024

TOKEN_TILE = 256
HALO = SUBLANES
CONV_COLS = 512
ML_CHUNK = TOKEN_TILE
ATTN_TQ = 512
ATTN_TK = 768


def _cparams(*sem):
    return pltpu.CompilerParams(dimension_semantics=sem, vmem_limit_bytes=VMEM_LIMIT)


def _silu(v):
    return v * jax.nn.sigmoid(v)


def _softplus(v):
    return jnp.maximum(v, 0.0) + jnp.log(1.0 + jnp.exp(-jnp.abs(v)))


def _log_sigmoid(v):
    return jnp.minimum(v, 0.0) - jnp.log(1.0 + jnp.exp(-jnp.abs(v)))


def _modulate(x, g, mod):
    ms = jnp.mean(x * x, axis=-1, keepdims=True)
    y = x * lax.rsqrt(ms + NORM_EPS) * g
    return y * (1.0 + mod[:, D:2 * D]) + mod[:, 0:D]


def _dot(a, b):
    return jnp.dot(a, b, preferred_element_type=F32)


def _dot_nt(a, b):
    return lax.dot_general(a, b, (((1,), (1,)), ((), ())), preferred_element_type=F32)


def _const_spec(shape):
    nd = len(shape)
    return pl.BlockSpec(shape, lambda *_: (0,) * nd, pipeline_mode=pl.Buffered(1))


def _mod_spec(per_batch):
    if per_batch:
        return pl.BlockSpec((1, 1, 3 * D), lambda b, i: (b, 0, 0))
    return pl.BlockSpec((1, 1, 3 * D), lambda b, i: (0, 0, 0))


def _adaln_body(cond_ref, w_ref, b_ref, o_ref):
    a = _silu(cond_ref[...]).astype(BF16)
    o_ref[0] = _dot(a, w_ref[0].astype(BF16)) + b_ref[0]


def adaln_all(cond, ada_w, ada_b):
    r = cond.shape[0]
    return pl.pallas_call(
        _adaln_body,
        grid=(DEPTH, 3),
        in_specs=[pl.BlockSpec((r, D), lambda i, j: (0, 0)),
                  pl.BlockSpec((1, D, D), lambda i, j: (i, 0, j)),
                  pl.BlockSpec((1, 1, D), lambda i, j: (i, 0, j))],
        out_specs=pl.BlockSpec((1, r, D), lambda i, j: (i, 0, j)),
        out_shape=jax.ShapeDtypeStruct((DEPTH, r, 3 * D), F32),
        compiler_params=_cparams("arbitrary", "arbitrary"),
        name="adaln",
    )(cond, ada_w, ada_b.reshape(DEPTH, 1, 3 * D))


def _halo_specs(tm, width, seq):
    r = tm // HALO
    last = seq // HALO - 1
    return [pl.BlockSpec((1, tm, width), lambda b, i: (b, i, 0)),
            pl.BlockSpec((1, HALO, width), lambda b, i: (b, jnp.maximum(i * r - 1, 0), 0)),
            pl.BlockSpec((1, HALO, width), lambda b, i: (b, jnp.minimum((i + 1) * r, last), 0))]


def _halo_rows(x_ref, xp_ref, xn_ref, g, mod):
    i = pl.program_id(1)
    h = _modulate(x_ref[0], g, mod)
    hp = jnp.where(i == 0, 0.0, _modulate(xp_ref[0], g, mod))
    hn = jnp.where(i == pl.num_programs(1) - 1, 0.0, _modulate(xn_ref[0], g, mod))
    return h, jnp.concatenate([hp, h, hn], axis=0).astype(BF16)


def _conv_silu(e, tm, w, b):
    n = e.shape[0]
    keep = slice(HALO, HALO + tm)
    y = (w[0:1, :] * pltpu.roll(e, 1, 0)[keep, :] + w[1:2, :] * e[keep, :]
         + w[2:3, :] * pltpu.roll(e, n - 1, 0)[keep, :] + w[3:4, :] * pltpu.roll(e, n - 2, 0)[keep, :] + b)
    return _silu(y)


def _rms(v, g):
    return v * lax.rsqrt(jnp.mean(v * v, axis=-1, keepdims=True) + NORM_EPS) * g


def _outproj_body(*refs, kind, final_norm):
    if kind == "mla":
        o_ref, z_ref, x_ref, mod_ref, w_ref = refs[:5]
        rest = refs[5:]
        o = jnp.concatenate([o_ref[0, k] for k in range(o_ref.shape[1])], axis=-1)
        pre = o.astype(F32) * _silu(z_ref[0].astype(F32))
    elif kind == "ssd":
        yf_ref, yb_ref, xs_ref, z_ref, x_ref, mod_ref, w_ref, dsk_ref, ng_ref = refs[:9]
        rest = refs[9:]
        y = yf_ref[0, 0].astype(F32) + yb_ref[0, 0].astype(F32) + dsk_ref[...] * xs_ref[0].astype(F32)
        pre = _rms(y * _silu(z_ref[0].astype(F32)), ng_ref[...])
    else:
        hf_ref, hb_ref, op_ref, xc_ref, z_ref, x_ref, mod_ref, w_ref, ng_ref, sk_ref = refs[:10]
        rest = refs[10:]
        hh = (hf_ref[0, 0].astype(F32) + hb_ref[0, 0].astype(F32)) * jax.nn.sigmoid(op_ref[0].astype(F32))
        parts = []
        for h in range(ML_HEADS):
            v = hh[:, h * ML_DH:(h + 1) * ML_DH]
            mu = jnp.mean(v, axis=-1, keepdims=True)
            vc = v - mu
            var = jnp.mean(vc * vc, axis=-1, keepdims=True)
            parts.append(vc * lax.rsqrt(var + NORM_EPS))
        hn = jnp.concatenate(parts, axis=-1) * ng_ref[...]
        pre = (hn + sk_ref[...] * xc_ref[0].astype(F32)) * _silu(z_ref[0].astype(F32))
    out = _dot(pre.astype(BF16), w_ref[...])
    xn = x_ref[0] + mod_ref[0][:, 2 * D:3 * D] * out
    if final_norm:
        fg_ref, xo_ref, yo_ref = rest
        xo_ref[0] = xn
        yo_ref[0] = _rms(xn, fg_ref[...])
    else:
        (xo_ref,) = rest
        xo_ref[0] = xn


def outproj(kind, x, mod, per_batch, w_out, tok_inputs, dir_inputs=(), consts=(), final_g=None):
    bsz, seq, _ = x.shape
    tm = TOKEN_TILE
    args, in_specs = [], []
    for a in dir_inputs:
        wd = a.shape[-1]
        for d in range(2):
            args.append(a)
            in_specs.append(pl.BlockSpec((1, 1, tm, wd), functools.partial(lambda b, i, d: (d, b, i, 0), d=d)))
    for a in tok_inputs:
        args.append(a)
        if a.ndim == 4:
            in_specs.append(pl.BlockSpec((1, a.shape[1], tm, a.shape[-1]), lambda b, i: (b, 0, i, 0)))
        else:
            in_specs.append(pl.BlockSpec((1, tm, a.shape[-1]), lambda b, i: (b, i, 0)))
    args += [x, mod, w_out]
    in_specs += [pl.BlockSpec((1, tm, D), lambda b, i: (b, i, 0)), _mod_spec(per_batch), _const_spec(w_out.shape)]
    for a in consts:
        args.append(a)
        in_specs.append(_const_spec(a.shape))
    tok_spec = pl.BlockSpec((1, tm, D), lambda b, i: (b, i, 0))
    shp = jax.ShapeDtypeStruct((bsz, seq, D), F32)
    if final_g is not None:
        args.append(final_g.reshape(1, D))
        in_specs.append(_const_spec((1, D)))
        out_specs, out_shape = [tok_spec, tok_spec], [shp, shp]
    else:
        out_specs, out_shape = [tok_spec], [shp]
    res = pl.pallas_call(
        functools.partial(_outproj_body, kind=kind, final_norm=final_g is not None),
        grid=(bsz, seq // tm),
        in_specs=in_specs, out_specs=out_specs, out_shape=out_shape,
        compiler_params=_cparams("parallel", "parallel"),
        name="outproj_" + kind,
    )(*args)
    return res if final_g is not None else res[0]


def _mla_pre_body(x_ref, g_ref, mod_ref, wcq_ref, wckv_ref, wkpe_ref, wkpes_ref, wz_ref, qg_ref, kvg_ref,
                  wuq1_ref, wuq2_ref, cq_ref, sq_ref, ck_ref, sk_ref,
                  q_ref, ckv_ref, kraw_ref, krope_ref, z_ref):
    h = _modulate(x_ref[0], g_ref[...], mod_ref[0]).astype(BF16)
    z_ref[0] = _dot(h, wz_ref[...]).astype(BF16)
    ckv_ref[0] = _rms(_dot(h, wckv_ref[...]), kvg_ref[...])
    kraw = _dot(h, wkpe_ref[...])
    kraw_ref[0] = kraw
    krope_ref[0] = kraw * ck_ref[...] + _dot(h, wkpes_ref[...]) * sk_ref[...]
    cq = _rms(_dot(h, wcq_ref[...]), qg_ref[...]).astype(BF16)
    cosq = cq_ref[...]
    sinq = sq_ref[...]
    for c0 in range(0, MLA_HEADS * LANES, 512):
        a = _dot(cq, wuq1_ref[:, c0:c0 + 512])
        b = _dot(cq, wuq2_ref[:, c0:c0 + 512])
        for s in range(0, 512, LANES):
            hd = (c0 + s) // LANES
            q_ref[0, hd // 2, :, (hd % 2) * LANES:(hd % 2 + 1) * LANES] = (
                a[:, s:s + LANES] * cosq + b[:, s:s + LANES] * sinq).astype(BF16)


def _head_slabs(w, off=0):
    k, nh, dh = w.shape
    return jnp.pad(w, ((0, 0), (0, 0), (off, LANES - off - dh))).reshape(k, nh * LANES)


def _swap_halves(w):
    half = w.shape[-1] // 2
    return jnp.concatenate([w[..., half:], w[..., :half]], axis=-1)


def _rope_tables(seq, use_rope):
    zeros32 = jnp.zeros((seq, LANES - MLA_NOPE - MLA_ROPE), F32)
    if use_rope:
        rows = seq // GRID_W
        row = jnp.repeat(jnp.arange(rows), GRID_W).astype(F32)
        col = jnp.tile(jnp.arange(GRID_W), rows).astype(F32)
        half = MLA_ROPE // 2
        inv = 1.0 / (ROPE_BASE ** (jnp.arange(0, half, 2, dtype=F32) / half))
        ang = jnp.concatenate([row[:, None] * inv, col[:, None] * inv], axis=-1)
        cos, sin = jnp.cos(ang), jnp.sin(ang)
    else:
        cos = jnp.ones((seq, MLA_ROPE // 2), F32)
        sin = jnp.zeros((seq, MLA_ROPE // 2), F32)
    ones64 = jnp.ones((seq, MLA_NOPE), F32)
    zeros64 = jnp.zeros((seq, MLA_NOPE), F32)
    cos_k = jnp.concatenate([zeros64, cos, cos, zeros32], axis=-1)
    sin_k = jnp.concatenate([zeros64, -sin, sin, zeros32], axis=-1)
    q_scale = MLA_SCALE * LOG2_E
    cos_q = jnp.concatenate([ones64, cos, cos, zeros32], axis=-1) * q_scale
    sin_q = sin_k * q_scale
    return cos_q, sin_q, cos_k, sin_k


def mla_pre(x, g, mod, per_batch, wts, tables):
    bsz, seq, _ = x.shape
    tm = TOKEN_TILE
    tok = lambda w, dt: (pl.BlockSpec((1, tm, w), lambda b, i: (b, i, 0)), jax.ShapeDtypeStruct((bsz, seq, w), dt))
    q_out = (pl.BlockSpec((1, MLA_HEADS // 2, tm, 2 * LANES), lambda b, i: (b, 0, i, 0)),
             jax.ShapeDtypeStruct((bsz, MLA_HEADS // 2, seq, 2 * LANES), BF16))
    outs = [q_out, tok(MLA_KV_RANK, F32), tok(LANES, F32), tok(LANES, F32), tok(D, BF16)]
    tab_spec = pl.BlockSpec((tm, LANES), lambda b, i: (i, 0))
    in_specs = ([pl.BlockSpec((1, tm, D), lambda b, i: (b, i, 0)), _const_spec((1, D)), _mod_spec(per_batch)]
                + [_const_spec(w.shape) for w in wts] + [tab_spec] * 4)
    return pl.pallas_call(
        _mla_pre_body,
        grid=(bsz, seq // tm),
        in_specs=in_specs,
        out_specs=[o[0] for o in outs], out_shape=[o[1] for o in outs],
        compiler_params=_cparams("parallel", "parallel"),
        name="mla_pre",
    )(x, g.reshape(1, D), mod, *wts, *tables)


def _kv_expand_body(ckv_ref, kpe_ref, wkt_ref, wv_ref, ones_ref, kt_ref, v_ref):
    c = ckv_ref[0].astype(BF16)
    v_ref[0] = (_dot(c, wv_ref[...]) + ones_ref[...]).astype(BF16)
    kpe_t = kpe_ref[0].T
    for c0 in range(0, MLA_HEADS * LANES, 512):
        a = _dot_nt(wkt_ref[c0:c0 + 512, :], c)
        for s in range(0, 512, LANES):
            kt_ref[0, 0, c0 + s:c0 + s + LANES, :] = (a[s:s + LANES, :] + kpe_t).astype(BF16)


def kv_expand(ckv, kpe_slab, wkt_aug, wv, tk):
    bsz, seq, _ = ckv.shape
    ones = (jnp.arange(MLA_HEADS * LANES) % LANES == MLA_V).astype(F32).reshape(1, -1)
    return pl.pallas_call(
        _kv_expand_body,
        grid=(bsz, seq // tk),
        in_specs=[pl.BlockSpec((1, tk, MLA_KV_RANK), lambda b, i: (b, i, 0)),
                  pl.BlockSpec((1, tk, LANES), lambda b, i: (b, i, 0)),
                  _const_spec(wkt_aug.shape), _const_spec(wv.shape), _const_spec(ones.shape)],
        out_specs=[pl.BlockSpec((1, 1, MLA_HEADS * LANES, tk), lambda b, i: (b, i, 0, 0)),
                   pl.BlockSpec((1, tk, MLA_HEADS * LANES), lambda b, i: (b, i, 0))],
        out_shape=[jax.ShapeDtypeStruct((bsz, seq // tk, MLA_HEADS * LANES, tk), BF16),
                   jax.ShapeDtypeStruct((bsz, seq, MLA_HEADS * LANES), BF16)],
        compiler_params=_cparams("parallel", "parallel"),
        name="mla_kv_expand",
    )(ckv, kpe_slab, wkt_aug, wv, ones)


ATTN_ROWS = 16


def _attn_body(q_ref, kt_ref, v_ref, o_ref, s_ref, p_ref, m_ref, a_ref, acc_ref, *, nk):
    tq = q_ref.shape[2]
    tk = kt_ref.shape[3]
    rb = ATTN_ROWS
    ncol = tk // LANES
    left = lax.broadcasted_iota(jnp.int32, (tq, LANES), 1) < MLA_V

    def scores(kc):
        for hh in range(2):
            s_ref[kc % 2, hh] = _dot(q_ref[0, 0, :, hh * LANES:(hh + 1) * LANES],
                                     kt_ref[0, kc, hh * LANES:(hh + 1) * LANES, :])

    def softmax(kc):
        sl = kc % 2
        for hh in range(2):
            for r in range(0, tq, rb):
                mx = s_ref[sl, hh, r:r + rb, 0:LANES]
                for cc in range(1, ncol):
                    mx = jnp.maximum(mx, s_ref[sl, hh, r:r + rb, cc * LANES:(cc + 1) * LANES])
                mx = jnp.broadcast_to(jnp.max(mx, axis=-1, keepdims=True), (rb, LANES))
                if kc > 0:
                    m_old = m_ref[hh, r:r + rb, :]
                    mx = jnp.maximum(m_old, mx)
                    a_ref[hh, r:r + rb, :] = jnp.exp2(m_old - mx)
                m_ref[hh, r:r + rb, :] = mx
            for r in range(0, tq, rb):
                m_new = m_ref[hh, r:r + rb, :]
                for cc in range(ncol):
                    p = jnp.exp2(s_ref[sl, hh, r:r + rb, cc * LANES:(cc + 1) * LANES] - m_new)
                    p_ref[hh, r:r + rb, cc * LANES:(cc + 1) * LANES] = p.astype(BF16)

    def values(kc):
        for hh in range(2):
            pv = _dot(p_ref[hh], v_ref[0, kc * tk:(kc + 1) * tk, hh * LANES:(hh + 1) * LANES])
            if kc > 0:
                pv = acc_ref[hh] * a_ref[hh] + pv
            acc_ref[hh] = pv

    scores(0)
    for kc in range(nk):
        if kc + 1 < nk:
            scores(kc + 1)
        softmax(kc)
        values(kc)
    outs = [acc_ref[hh] / acc_ref[hh, :, MLA_V:MLA_V + 1] for hh in range(2)]
    o_ref[0, 0] = jnp.where(left, outs[0], pltpu.roll(outs[1], MLA_V, 1)).astype(BF16)


def mla_attention(q_aug, kt, v):
    bsz, _, seq, _ = q_aug.shape
    _, nk, _, tk = kt.shape
    lk = nk * tk
    tq = min(ATTN_TQ, seq)
    stat = pltpu.VMEM((2, tq, LANES), F32)
    return pl.pallas_call(
        functools.partial(_attn_body, nk=nk),
        grid=(bsz, MLA_HEADS // 2, seq // tq),
        in_specs=[pl.BlockSpec((1, 1, tq, 2 * LANES), lambda b, h, i: (b, h, i, 0)),
                  pl.BlockSpec((1, nk, 2 * LANES, tk), lambda b, h, i: (b, 0, h, 0)),
                  pl.BlockSpec((1, lk, 2 * LANES), lambda b, h, i: (b, 0, h))],
        out_specs=pl.BlockSpec((1, 1, tq, LANES), lambda b, h, i: (b, h, i, 0)),
        out_shape=jax.ShapeDtypeStruct((bsz, MLA_HEADS // 2, seq, LANES), BF16),
        scratch_shapes=[pltpu.VMEM((2, 2, tq, tk), F32), pltpu.VMEM((2, tq, tk), BF16), stat, stat, stat],
        compiler_params=_cparams("parallel", "parallel", "arbitrary"),
        name="mla_attention",
    )(q_aug, kt, v)


def _mla_weights(w_in, q_norm_g, kv_norm_g, w_uq, w_uk, w_uv):
    o1, o2, o3 = MLA_Q_RANK, MLA_Q_RANK + MLA_KV_RANK, MLA_Q_RANK + MLA_KV_RANK + MLA_ROPE
    w_kpe = w_in[:, o2:o3]
    pad_k = lambda w: jnp.pad(w, ((0, 0), (MLA_NOPE, LANES - MLA_NOPE - MLA_ROPE)))
    uq_pe = w_uq[..., MLA_NOPE:]
    wuq1 = _head_slabs(w_uq)
    wuq2 = _head_slabs(_swap_halves(uq_pe), off=MLA_NOPE)
    pre = (w_in[:, :o1].astype(BF16), w_in[:, o1:o2].astype(BF16), pad_k(w_kpe).astype(BF16),
           pad_k(_swap_halves(w_kpe)).astype(BF16), w_in[:, o3:].astype(BF16),
           q_norm_g.reshape(1, -1), kv_norm_g.reshape(1, -1), wuq1.astype(BF16), wuq2.astype(BF16))
    wkt_aug = _head_slabs(w_uk).T.astype(BF16)
    wv = _head_slabs(w_uv).astype(BF16)
    return pre, wkt_aug, wv


def _key_chunk(lk):
    tk = min(ATTN_TK, lk)
    while lk % tk:
        tk -= LANES
    return tk


def mla_layer(x, g, mod, per_batch, wts, w_out, use_rope, ctx=None, final_g=None):
    pre_w, wkt_aug, wv = wts
    seq = x.shape[1]
    q_aug, ckv, kraw, krope, z = mla_pre(x, g, mod, per_batch, pre_w, _rope_tables(seq, use_rope))
    ckv_all, kpe_all = ckv, krope
    if ctx is not None:
        ctx_ckv, ctx_kpe = ctx
        ckv_all = jnp.concatenate([ckv, ctx_ckv], axis=1)
        kpe_all = jnp.concatenate(
            [krope, jnp.pad(ctx_kpe, ((0, 0), (0, 0), (MLA_NOPE, LANES - MLA_NOPE - MLA_ROPE)))], axis=1)
    kt, v = kv_expand(ckv_all, kpe_all, wkt_aug, wv, _key_chunk(ckv_all.shape[1]))
    o = mla_attention(q_aug, kt, v)
    res = outproj("mla", x, mod, per_batch, w_out, [o, z], final_g=final_g)
    return res, ckv, kraw[..., MLA_NOPE:MLA_NOPE + MLA_ROPE]


DT_REP = 4 * SSD_HEADS


def _ssd_in_body(x_ref, xp_ref, xn_ref, g_ref, mod_ref, wz_ref, wx_ref, wdt_ref, bdt_ref, wdtt_ref, bdtt_ref,
                 cw_ref, cb_ref, z_ref, xs_ref, bt_ref, c_ref, dt_ref, dtt_ref):
    tm = x_ref.shape[1]
    h, h_ext = _halo_rows(x_ref, xp_ref, xn_ref, g_ref[...], mod_ref[0])
    hb = h.astype(BF16)
    for c0 in range(0, SSD_INNER, 1024):
        z_ref[0, :, c0:c0 + 1024] = _dot(hb, wz_ref[:, c0:c0 + 1024]).astype(BF16)
    dt_ref[0] = _softplus(_dot(hb, wdt_ref[...]) + bdt_ref[...])
    dtt_ref[0] = _softplus(_dot_nt(wdtt_ref[...], hb) + bdtt_ref[...])
    cw = CONV_COLS
    for c0 in range(0, SSD_CONV_DIM, cw):
        e = _dot(h_ext, wx_ref[:, c0:c0 + cw])
        y = _conv_silu(e, tm, cw_ref[:, c0:c0 + cw], cb_ref[:, c0:c0 + cw])
        if c0 < SSD_INNER:
            xs_ref[0, :, c0:c0 + cw] = y.astype(BF16)
        elif c0 < SSD_INNER + SSD_BC:
            o = c0 - SSD_INNER
            y_t = y.T.astype(BF16)
            for ch in range(tm // SSD_CHUNK):
                bt_ref[0, ch, o:o + cw, :] = y_t[:, ch * SSD_CHUNK:(ch + 1) * SSD_CHUNK]
        else:
            o = c0 - SSD_INNER - SSD_BC
            c_ref[0, :, o:o + cw] = y.astype(BF16)


def ssd_in(x, g, mod, per_batch, operands):
    bsz, seq, _ = x.shape
    tm = TOKEN_TILE
    tok = lambda w: pl.BlockSpec((1, tm, w), lambda b, i: (b, i, 0))
    return pl.pallas_call(
        _ssd_in_body,
        grid=(bsz, seq // tm),
        in_specs=_halo_specs(tm, D, seq) + [_const_spec((1, D)), _mod_spec(per_batch)]
        + [_const_spec(op.shape) for op in operands],
        out_specs=[tok(SSD_INNER), tok(SSD_INNER),
                   pl.BlockSpec((1, tm // SSD_CHUNK, SSD_BC, SSD_CHUNK), lambda b, i: (b, i, 0, 0)),
                   tok(SSD_BC), tok(2 * DT_REP), pl.BlockSpec((1, 2 * SSD_HEADS, tm), lambda b, i: (b, 0, i))],
        out_shape=[jax.ShapeDtypeStruct((bsz, seq, SSD_INNER), BF16),
                   jax.ShapeDtypeStruct((bsz, seq, SSD_INNER), BF16),
                   jax.ShapeDtypeStruct((bsz, seq // SSD_CHUNK, SSD_BC, SSD_CHUNK), BF16),
                   jax.ShapeDtypeStruct((bsz, seq, SSD_BC), BF16),
                   jax.ShapeDtypeStruct((bsz, seq, 2 * DT_REP), F32),
                   jax.ShapeDtypeStruct((bsz, 2 * SSD_HEADS, seq), F32)],
        compiler_params=_cparams("parallel", "parallel"),
        name="ssd_in",
    )(x, x, x, g.reshape(1, D), mod, *operands)


def _scan_masks(d, n):
    ri = lax.broadcasted_iota(jnp.int32, (n, n), 0)
    ci = lax.broadcasted_iota(jnp.int32, (n, n), 1)
    diff = (ri - ci) * (1 - 2 * d)
    return diff >= 0, diff <= 0


def _split3(v):
    lane = lax.broadcasted_iota(jnp.int32, v.shape, 1)
    hi = v.astype(BF16).astype(F32)
    r1 = v - hi
    mid = r1.astype(BF16).astype(F32)
    lo = r1 - mid
    w = SSD_HEADS
    return jnp.where(lane < w, hi, jnp.where(lane < 2 * w, mid, jnp.where(lane < 3 * w, lo, 0.0))).astype(BF16)


def _expand_matrix(width):
    k = jnp.arange(DT_REP)[:, None]
    col_head = jnp.arange(SSD_HEADS * width)[None, :] // width
    return ((k < 3 * SSD_HEADS) & (k % SSD_HEADS == col_head)).astype(BF16)


def _ssd_scan_body(*refs, has_h0):
    if has_h0:
        xs_ref, bt_ref, c_ref, dt_ref, dtt_ref, a_ref, at_ref, eq_ref, ep_ref, h0_ref, y_ref, hout_ref, ht_s = refs
    else:
        xs_ref, bt_ref, c_ref, dt_ref, dtt_ref, a_ref, at_ref, eq_ref, ep_ref, y_ref, hout_ref, ht_s = refs
    d = pl.program_id(1)
    c = pl.program_id(2)
    q = SSD_CHUNK
    hp_n = SSD_HEADS * SSD_HEADDIM

    @pl.when(c == 0)
    def _():
        if has_h0:
            ht_s[...] = h0_ref[0, 0].reshape(hp_n, SSD_STATE).T
        else:
            ht_s[...] = jnp.zeros_like(ht_s)

    mask, mask_t = _scan_masks(d, q)
    dt = dt_ref[0]
    dtt = dtt_ref[0]
    da = dt * a_ref[0]
    dat = dtt * at_ref[0]
    cum = jnp.dot(mask.astype(F32), da, precision=HIGHEST, preferred_element_type=F32)
    cum_t = jnp.dot(dat, mask_t.astype(F32), precision=HIGHEST, preferred_element_type=F32)
    tot = jnp.sum(da, axis=0, keepdims=True)
    cum3 = _split3(cum)
    ecum3 = _split3(jnp.exp(cum))
    wend3 = _split3(jnp.exp(tot - cum) * dt)
    left = lax.broadcasted_iota(jnp.int32, (q, LANES), 1) < SSD_HEADDIM
    heads_g = SSD_HEADS // SSD_GROUPS
    gw = heads_g * SSD_HEADDIM

    for g in range(SSD_GROUPS):
        cum_b = _dot(cum3, eq_ref[:, g * heads_g * q:(g + 1) * heads_g * q])
        ecum_b = _dot(ecum3, ep_ref[:, g * gw:(g + 1) * gw])
        wend_b = _dot(wend3, ep_ref[:, g * gw:(g + 1) * gw])
        decay_b = jnp.where(d == 0, ecum_b[q - 1:q, :], ecum_b[0:1, :])
        cg = c_ref[0, :, g * SSD_STATE:(g + 1) * SSD_STATE]
        btg = bt_ref[0, 0, g * SSD_STATE:(g + 1) * SSD_STATE, :]
        cb = _dot(cg, btg)
        y_state = _dot(cg, ht_s[:, g * gw:(g + 1) * gw].astype(BF16))
        for pr in range(gw // LANES):
            lo = g * gw + pr * LANES
            sl = slice(pr * LANES, (pr + 1) * LANES)
            xs_b = xs_ref[0, :, lo:lo + LANES]
            xs = xs_b.astype(F32)
            ys = []
            for hh in range(2):
                hl = 2 * pr + hh
                h = g * heads_g + hl
                seg = cum_b[:, hl * q:(hl + 1) * q] - cum_t[h:h + 1, :]
                lm = jnp.exp(jnp.where(mask, seg, -jnp.inf))
                mm = (cb * lm * dtt[h:h + 1, :]).astype(BF16)
                ys.append(_dot(mm, xs_b))
            y_ref[0, 0, :, lo:lo + LANES] = (jnp.where(left, ys[0], ys[1])
                                             + y_state[:, sl] * ecum_b[:, sl]).astype(BF16)
            xw = (xs * wend_b[:, sl]).astype(BF16)
            ht_s[:, lo:lo + LANES] = ht_s[:, lo:lo + LANES] * decay_b[:, sl] + _dot(btg, xw)

    @pl.when(c == pl.num_programs(2) - 1)
    def _():
        hout_ref[0, 0] = ht_s[...].T.reshape(SSD_HEADS, SSD_HEADDIM, SSD_STATE)


def ssd_scan(xs, bt, cm, dt_rep, dtt, a, h0):
    bsz, seq, _ = xs.shape
    q = SSD_CHUNK
    nc = seq // q
    cidx = lambda d, c: c + d * (nc - 1 - 2 * c)
    a_rep = jnp.concatenate([a, a, a, jnp.zeros_like(a)], axis=-1).reshape(2, 1, DT_REP)
    eq = _expand_matrix(q)
    ep = _expand_matrix(SSD_HEADDIM)
    in_specs = [pl.BlockSpec((1, q, SSD_INNER), lambda b, d, c: (b, cidx(d, c), 0)),
                pl.BlockSpec((1, 1, SSD_BC, q), lambda b, d, c: (b, cidx(d, c), 0, 0)),
                pl.BlockSpec((1, q, SSD_BC), lambda b, d, c: (b, cidx(d, c), 0)),
                pl.BlockSpec((1, q, DT_REP), lambda b, d, c: (b, cidx(d, c), d)),
                pl.BlockSpec((1, SSD_HEADS, q), lambda b, d, c: (b, d, cidx(d, c))),
                pl.BlockSpec((1, 1, DT_REP), lambda b, d, c: (d, 0, 0)),
                pl.BlockSpec((1, SSD_HEADS, 1), lambda b, d, c: (d, 0, 0)),
                _const_spec(eq.shape), _const_spec(ep.shape)]
    args = [xs, bt, cm, dt_rep, dtt, a_rep, a.reshape(2, SSD_HEADS, 1), eq, ep]
    st_spec = pl.BlockSpec((1, 1, SSD_HEADS, SSD_HEADDIM, SSD_STATE), lambda b, d, c: (b, d, 0, 0, 0))
    if h0 is not None:
        in_specs.append(st_spec)
        args.append(h0)
    return pl.pallas_call(
        functools.partial(_ssd_scan_body, has_h0=h0 is not None),
        grid=(bsz, 2, nc),
        in_specs=in_specs,
        out_specs=[pl.BlockSpec((1, 1, q, SSD_INNER), lambda b, d, c: (d, b, cidx(d, c), 0)), st_spec],
        out_shape=[jax.ShapeDtypeStruct((2, bsz, seq, SSD_INNER), BF16),
                   jax.ShapeDtypeStruct((bsz, 2, SSD_HEADS, SSD_HEADDIM, SSD_STATE), F32)],
        scratch_shapes=[pltpu.VMEM((SSD_STATE, SSD_INNER), F32)],
        compiler_params=_cparams("parallel", "arbitrary", "arbitrary"),
        name="ssd_scan",
    )(*args)


def _ssd_weights(w_in, conv_w, conv_b, dt_bias, a_log, d_skip, norm_g, w_out):
    o1, o2 = SSD_INNER, SSD_INNER + SSD_CONV_DIM
    w_dt = w_in[:, o2:]
    nh = SSD_HEADS
    rep = lambda t: jnp.concatenate([t[..., :nh]] * 3 + [jnp.zeros_like(t[..., :nh])]
                                    + [t[..., nh:]] * 3 + [jnp.zeros_like(t[..., :nh])], axis=-1)
    operands = (w_in[:, :o1].astype(BF16), w_in[:, o1:o2].astype(BF16),
                rep(w_dt).astype(BF16), rep(dt_bias.reshape(1, -1)),
                w_dt.T.astype(BF16), dt_bias.reshape(-1, 1),
                conv_w, conv_b.reshape(1, SSD_CONV_DIM))
    a = -jnp.exp(a_log.astype(F32))
    dsk = jnp.repeat(d_skip, SSD_HEADDIM).reshape(1, SSD_INNER)
    return operands, a, dsk, norm_g.reshape(1, SSD_INNER), w_out.astype(BF16)


def ssd_layer(x, g, mod, per_batch, wts, h0):
    operands, a, dsk, norm_g, w_out = wts
    z, xs, bt, cm, dt_rep, dtt = ssd_in(x, g, mod, per_batch, operands)
    y, h_new = ssd_scan(xs, bt, cm, dt_rep, dtt, a, h0)
    xn = outproj("ssd", x, mod, per_batch, w_out, [xs, z], dir_inputs=[y], consts=[dsk, norm_g])
    return xn, h_new


def _ml_gate_act(r, transposed):
    if transposed:
        idx = lax.broadcasted_iota(jnp.int32, r.shape, 0)
    else:
        idx = lax.broadcasted_iota(jnp.int32, r.shape, 1) % LANES
    return jnp.where((idx // ML_HEADS) % 2 == 1, _log_sigmoid(r), r)


def _ml_in_body(x_ref, xp_ref, xn_ref, g_ref, mod_ref, wxm_ref, wo_ref, wz_ref, wg_ref, bg_ref, wgt_ref, bgt_ref,
                cw_ref, cb_ref, wq_ref, wkt_ref, wv_ref,
                xc_ref, q_ref, kt_ref, v_ref, o_ref, z_ref, gate_ref, gatet_ref):
    tm = x_ref.shape[1]
    h, h_ext = _halo_rows(x_ref, xp_ref, xn_ref, g_ref[...], mod_ref[0])
    hb = h.astype(BF16)
    for c0 in range(0, ML_INNER, 1024):
        o_ref[0, :, c0:c0 + 1024] = _dot(hb, wo_ref[:, c0:c0 + 1024]).astype(BF16)
        z_ref[0, :, c0:c0 + 1024] = _dot(hb, wz_ref[:, c0:c0 + 1024]).astype(BF16)
    gate_ref[0] = _ml_gate_act(_dot(hb, wg_ref[...]) + bg_ref[...], False)
    gatet_ref[0] = _ml_gate_act(_dot_nt(wgt_ref[...], hb) + bgt_ref[...], True)
    for hd in range(ML_HEADS):
        sl = slice(hd * ML_DH, (hd + 1) * ML_DH)
        e = _dot(h_ext, wxm_ref[:, sl])
        xm = e[HALO:tm + HALO, :]
        xc_b = _conv_silu(e, tm, cw_ref[:, sl], cb_ref[:, sl]).astype(BF16)
        xc_ref[0, :, sl] = xc_b
        q_ref[0, :, sl] = _dot(xc_b, wq_ref[hd]).astype(BF16)
        kt_ref[0, 0, hd] = (_dot_nt(wkt_ref[hd], xc_b) * (ML_DH ** -0.5)).astype(BF16)
        v_ref[0, :, sl] = _dot(xm.astype(BF16), wv_ref[hd]).astype(BF16)


def ml_in(x, g, mod, per_batch, operands):
    bsz, seq, _ = x.shape
    tm = TOKEN_TILE
    tok = pl.BlockSpec((1, tm, ML_INNER), lambda b, i: (b, i, 0))
    big = lambda dt: jax.ShapeDtypeStruct((bsz, seq, ML_INNER), dt)
    ng = 4 * ML_HEADS
    return pl.pallas_call(
        _ml_in_body,
        grid=(bsz, seq // tm),
        in_specs=_halo_specs(tm, D, seq) + [_const_spec((1, D)), _mod_spec(per_batch)]
        + [_const_spec(op.shape) for op in operands],
        out_specs=[tok, tok, pl.BlockSpec((1, 1, ML_HEADS, ML_DH, tm), lambda b, i: (b, i, 0, 0, 0)), tok, tok, tok,
                   pl.BlockSpec((1, tm, 2 * LANES), lambda b, i: (b, i, 0)),
                   pl.BlockSpec((1, ng, tm), lambda b, i: (b, 0, i))],
        out_shape=[big(BF16), big(BF16), jax.ShapeDtypeStruct((bsz, seq // tm, ML_HEADS, ML_DH, tm), BF16), big(BF16),
                   big(BF16), big(BF16), jax.ShapeDtypeStruct((bsz, seq, 2 * LANES), F32),
                   jax.ShapeDtypeStruct((bsz, ng, seq), F32)],
        compiler_params=_cparams("parallel", "parallel"),
        name="ml_in",
    )(x, x, x, g.reshape(1, D), mod, *operands)


def _ml_scan_body(*refs, has_state):
    if has_state:
        q_ref, kt_ref, v_ref, g_ref, gt_ref, c0_ref, n0_ref, m0_ref, h_ref, c_ref, n_ref, m_ref = refs
    else:
        q_ref, kt_ref, v_ref, g_ref, gt_ref, h_ref, c_ref, n_ref, m_ref = refs
    d = pl.program_id(1)
    c = pl.program_id(2)
    lc = q_ref.shape[1]
    nh = ML_HEADS

    @pl.when(c == 0)
    def _():
        if has_state:
            c_ref[...] = c0_ref[...]
            n_ref[...] = n0_ref[...]
            m_ref[...] = m0_ref[...]
        else:
            c_ref[...] = jnp.zeros_like(c_ref)
            n_ref[...] = jnp.zeros_like(n_ref)
            m_ref[...] = jnp.zeros_like(m_ref)

    mask, mask_t = _scan_masks(d, lc)
    fwd = d == 0
    gates = g_ref[0]
    gates_t = gt_ref[0]
    lf = gates[:, nh:2 * nh]
    lf_t = gates_t[nh:2 * nh, :]
    cum = jnp.dot(mask.astype(F32), lf, precision=HIGHEST, preferred_element_type=F32)
    cum_t = jnp.dot(lf_t, mask_t.astype(F32), precision=HIGHEST, preferred_element_type=F32)
    tot = jnp.sum(lf, axis=0, keepdims=True)
    for h in range(nh):
        sl = slice(h * ML_DH, (h + 1) * ML_DH)
        bcol = cum[:, h:h + 1]
        brow = cum_t[h:h + 1, :]
        irow = gates_t[h:h + 1, :]
        m_prev = m_ref[0, 0, h:h + 1, 0:1]
        dm = jnp.where(mask, bcol - brow + irow, -jnp.inf)
        m_inter = bcol + m_prev
        m_i = jnp.maximum(jnp.max(dm, axis=-1, keepdims=True), m_inter)
        wmat = jnp.exp(dm - m_i)
        g_inter = jnp.exp(m_inter - m_i)
        qh = q_ref[0, :, sl]
        kth = kt_ref[0, 0, h]
        vh = v_ref[0, :, sl]
        s = _dot(qh, kth) * wmat
        c_old = c_ref[0, 0, h]
        n_old = n_ref[0, 0, h:h + 1, :]
        num = _dot(s.astype(BF16), vh) + g_inter * _dot(qh, c_old.astype(BF16))
        qn = _dot_nt(qh, jnp.broadcast_to(n_old, (SUBLANES, ML_DH)).astype(BF16))[:, 0:1]
        den = jnp.sum(s, axis=-1, keepdims=True) + g_inter * qn
        denom = jnp.maximum(jnp.abs(den), jnp.exp(-m_i))
        h_ref[0, 0, :, sl] = (num / denom).astype(BF16)
        m_new = jnp.where(fwd, m_i[lc - 1:lc, :], m_i[0:1, :])
        tot_h = tot[:, h:h + 1]
        w_row = jnp.exp(tot_h - brow + irow - m_new)
        decay = jnp.exp(tot_h + m_prev - m_new)
        kw = (kth.astype(F32) * w_row).astype(BF16)
        c_ref[0, 0, h] = decay * c_old + _dot(kw, vh)
        w8 = jnp.broadcast_to(w_row, (SUBLANES, lc)).astype(BF16)
        n_ref[0, 0, h:h + 1, :] = decay * n_old + _dot_nt(w8, kth)[0:1, :]
        m_ref[0, 0, h:h + 1, :] = jnp.broadcast_to(m_new, (1, LANES))


def ml_scan(q, kt, v, gates, gates_t, state):
    bsz, seq, _ = q.shape
    lc = kt.shape[-1]
    nc = seq // lc
    cidx = lambda d, c: c + d * (nc - 1 - 2 * c)
    tok = pl.BlockSpec((1, lc, ML_INNER), lambda b, d, c: (b, cidx(d, c), 0))
    in_specs = [tok,
                pl.BlockSpec((1, 1, ML_HEADS, ML_DH, lc), lambda b, d, c: (b, cidx(d, c), 0, 0, 0)),
                tok,
                pl.BlockSpec((1, lc, LANES), lambda b, d, c: (b, cidx(d, c), d)),
                pl.BlockSpec((1, 2 * ML_HEADS, lc), lambda b, d, c: (b, d, cidx(d, c)))]
    args = [q, kt, v, gates, gates_t]
    st_specs = [pl.BlockSpec((1, 1, ML_HEADS, ML_DH, ML_DH), lambda b, d, c: (b, d, 0, 0, 0)),
                pl.BlockSpec((1, 1, ML_HEADS, ML_DH), lambda b, d, c: (b, d, 0, 0)),
                pl.BlockSpec((1, 1, SUBLANES, LANES), lambda b, d, c: (b, d, 0, 0))]
    if state is not None:
        in_specs += st_specs
        args += list(state)
    return pl.pallas_call(
        functools.partial(_ml_scan_body, has_state=state is not None),
        grid=(bsz, 2, nc),
        in_specs=in_specs,
        out_specs=[pl.BlockSpec((1, 1, lc, ML_INNER), lambda b, d, c: (d, b, cidx(d, c), 0))] + st_specs,
        out_shape=[jax.ShapeDtypeStruct((2, bsz, seq, ML_INNER), BF16),
                   jax.ShapeDtypeStruct((bsz, 2, ML_HEADS, ML_DH, ML_DH), F32),
                   jax.ShapeDtypeStruct((bsz, 2, ML_HEADS, ML_DH), F32),
                   jax.ShapeDtypeStruct((bsz, 2, SUBLANES, LANES), F32)],
        compiler_params=_cparams("parallel", "arbitrary", "arbitrary"),
        name="ml_scan",
    )(*args)


def _ml_weights(w_in, gate_b, conv_w, conv_b, w_q, w_k, w_v, norm_g, skip, w_out):
    w_g = w_in[:, 3 * ML_INNER:]
    half = 2 * ML_HEADS
    pad = lambda t: jnp.pad(t, ((0, 0), (0, LANES - half)))
    rep = lambda t: jnp.concatenate([pad(t[:, :half]), pad(t[:, half:])], axis=-1)
    operands = (w_in[:, :ML_INNER].astype(BF16), w_in[:, ML_INNER:2 * ML_INNER].astype(BF16),
                w_in[:, 2 * ML_INNER:3 * ML_INNER].astype(BF16),
                rep(w_g).astype(BF16), rep(gate_b.reshape(1, -1)), w_g.T.astype(BF16), gate_b.reshape(-1, 1),
                conv_w, conv_b.reshape(1, ML_INNER),
                w_q.astype(BF16), jnp.swapaxes(w_k, 1, 2).astype(BF16), w_v.astype(BF16))
    return operands, norm_g.reshape(1, ML_INNER), skip.reshape(1, ML_INNER), w_out.astype(BF16)


def ml_layer(x, g, mod, per_batch, wts, state):
    operands, norm_g, skip, w_out = wts
    xc, q, kt, v, o_pre, z, gates, gates_t = ml_in(x, g, mod, per_batch, operands)
    if state is not None:
        c0, n0, m0 = state
        m0 = jnp.broadcast_to(jnp.pad(m0, ((0, 0), (0, 0), (0, SUBLANES - ML_HEADS)))[..., None],
                              m0.shape[:2] + (SUBLANES, LANES))
        state = (c0, n0, m0)
    hs, c_new, n_new, m_new = ml_scan(q, kt, v, gates, gates_t, state)
    xn = outproj("ml", x, mod, per_batch, w_out, [o_pre, xc, z], dir_inputs=[hs], consts=[norm_g, skip])
    return xn, c_new, n_new, m_new[:, :, :ML_HEADS, 0]


def kernel(x_prompt, x_sample, cache_mla_ckv, cache_mla_kpe, state_ssd, state_mlstm_c, state_mlstm_n,
           state_mlstm_m, c, c_ctx, ada_w, ada_b, norm_g, final_norm_g,
           mla_w_in, mla_q_norm_g, mla_kv_norm_g, mla_w_uq, mla_w_uk, mla_w_uv, mla_w_out,
           ssd_w_in, ssd_conv_w, ssd_conv_b, ssd_dt_bias, ssd_a_log, ssd_d, ssd_norm_g, ssd_w_out,
           ml_w_in, ml_gate_b, ml_conv_w, ml_conv_b, ml_w_q, ml_w_k, ml_w_v, ml_norm_g, ml_skip, ml_w_out):
    nb = c.shape[0]
    rows = -(-(nb + 1) // SUBLANES) * SUBLANES
    cond = jnp.zeros((rows, D), F32).at[0].set(c_ctx).at[1:nb + 1].set(c)
    mods = adaln_all(cond, ada_w, ada_b)
    xp, xs = x_prompt, x_sample
    new_ckv, new_kpe, new_ssd, new_c, new_n, new_m = [], [], [], [], [], []
    yp = ys = None
    for i in range(DEPTH):
        kind, j = i % N_MIXERS, i // N_MIXERS
        mod_p = mods[i, 0:1].reshape(1, 1, 3 * D)
        mod_s = mods[i, 1:nb + 1].reshape(nb, 1, 3 * D)
        fg = final_norm_g if i == DEPTH - 1 else None
        if kind == 0:
            wts = _mla_weights(mla_w_in[j], mla_q_norm_g[j], mla_kv_norm_g[j], mla_w_uq[j], mla_w_uk[j], mla_w_uv[j])
            w_out = mla_w_out[j].astype(BF16)
            rp, ckv, kpe = mla_layer(xp, norm_g[i], mod_p, False, wts, w_out, False, final_g=fg)
            rs, _, _ = mla_layer(xs, norm_g[i], mod_s, True, wts, w_out, True,
                                 ctx=(cache_mla_ckv[:, j], cache_mla_kpe[:, j]), final_g=fg)
            new_ckv.append(ckv)
            new_kpe.append(kpe)
        elif kind == 1:
            wts = _ssd_weights(ssd_w_in[j], ssd_conv_w[j], ssd_conv_b[j], ssd_dt_bias[j], ssd_a_log[j], ssd_d[j],
                               ssd_norm_g[j], ssd_w_out[j])
            rp, st = ssd_layer(xp, norm_g[i], mod_p, False, wts, None)
            rs, _ = ssd_layer(xs, norm_g[i], mod_s, True, wts, state_ssd[:, j])
            new_ssd.append(st)
        else:
            wts = _ml_weights(ml_w_in[j], ml_gate_b[j], ml_conv_w[j], ml_conv_b[j], ml_w_q[j], ml_w_k[j], ml_w_v[j],
                              ml_norm_g[j], ml_skip[j], ml_w_out[j])
            rp, cc, nn, mm = ml_layer(xp, norm_g[i], mod_p, False, wts, None)
            rs, _, _, _ = ml_layer(xs, norm_g[i], mod_s, True, wts,
                                   (state_mlstm_c[:, j], state_mlstm_n[:, j], state_mlstm_m[:, j]))
            new_c.append(cc)
            new_n.append(nn)
            new_m.append(mm)
        if fg is not None:
            (xp, yp), (xs, ys) = rp, rs
        else:
            xp, xs = rp, rs
    return (yp, ys, jnp.stack(new_ckv, axis=1), jnp.stack(new_kpe, axis=1), jnp.stack(new_ssd, axis=1),
            jnp.stack(new_c, axis=1), jnp.stack(new_n, axis=1), jnp.stack(new_m, axis=1))
```
